```python
import jax
import jax.numpy as jnp
from jax import lax
import numpy as np

D_MODEL = 2048
BATCH = 8
SEQ = 2048
DEPTH = 2
DEC_BATCH = 8
DEC_SEQ = 64
PAST_LEN = 2048

CHUNK = 64
N_BRANCH = 4
MIX_W = D_MODEL // 4
D_FF = 4 * D_MODEL
NORM_EPS = 1e-6

A_HD = 64
A_HEADS = MIX_W // A_HD
A_DECAY_RANK = 64
A_ICL_RANK = 64
A_GATE_RANK = 128
A_DECAY_SCALE = 0.6065306597
A_GN_EPS = 64e-5
A_IN = 3 * MIX_W + A_DECAY_RANK + A_ICL_RANK + A_GATE_RANK

B_HEADS = 4
B_DK = MIX_W // (2 * B_HEADS)
B_DV = MIX_W // B_HEADS
B_GATE_RANK = 16
B_GATE_TEMP = 16.0
B_IN = 2 * B_HEADS * B_DK + MIX_W + B_GATE_RANK + MIX_W

C_HD = 64
C_HEADS = MIX_W // C_HD
C_BAND_CHUNKS = 8
C_REL_CLIP = 2 * CHUNK
C_IN = 3 * MIX_W

D_HD = 128
D_HEADS = MIX_W // D_HD
D_CONV = 4
D_IN = 3 * MIX_W + 2 * D_HEADS + MIX_W

IN_TOTAL = A_IN + B_IN + C_IN + D_IN

kernel_name = 'hybrid_streaming_encoder_step'


def rms_norm(x, gain, eps=NORM_EPS):
    xf = x.astype(jnp.float32)
    xf = xf * lax.rsqrt(jnp.mean(xf * xf, axis=-1, keepdims=True) + eps)
    return (xf * gain.astype(jnp.float32)).astype(x.dtype)


def l2_normalize(x, eps=1e-6):
    x = x.astype(jnp.float32)
    return x * lax.rsqrt(jnp.sum(x * x, axis=-1, keepdims=True) + eps)


def split_cols(z, sizes):
    parts, start = [], 0
    for size in sizes:
        parts.append(z[..., start:start + size])
        start += size
    return parts


def rwkv7_mix(p, shift_prev, s0, mu, w0, w_up, a0, a_up, g_up, k_k, k_a, r_k, ln_w, ln_b):
    bsz, t, _ = p.shape
    dt = p.dtype
    prev = jnp.concatenate([shift_prev.astype(dt), p[:, :-1]], axis=1)
    xs = p + (prev - p) * mu
    r, k, v, wl, al, gl = split_cols(xs, (MIX_W, MIX_W, MIX_W, A_DECAY_RANK, A_ICL_RANK, A_GATE_RANK))
    log_w = -A_DECAY_SCALE * jax.nn.sigmoid((w0 + jnp.tanh(wl) @ w_up).astype(jnp.float32))
    a = jax.nn.sigmoid((a0 + al @ a_up).astype(jnp.float32))
    g = jax.nn.sigmoid(gl) @ g_up

    def heads(z):
        return z.reshape(bsz, t, A_HEADS, A_HD).astype(jnp.float32)

    kk = l2_normalize(heads(k * k_k))
    a_h = heads(a)
    k_h = heads(k * (1.0 + (a - 1.0) * k_a))
    r_h, v_h, decay = heads(r), heads(v), jnp.exp(heads(log_w))

    def step(s, inp):
        r_t, k_t, v_t, kk_t, a_t, w_t = inp
        s_kk = jnp.einsum('bhvk,bhk->bhv', s, kk_t)
        s = (s * w_t[:, :, None, :] - s_kk[..., None] * (kk_t * a_t)[:, :, None, :]
             + v_t[..., None] * k_t[:, :, None, :])
        return s, jnp.einsum('bhvk,bhk->bhv', s, r_t)

    def tm(z):
        return jnp.moveaxis(z, 1, 0)

    s_fin, y = lax.scan(step, s0.astype(jnp.float32),
                        (tm(r_h), tm(k_h), tm(v_h), tm(kk), tm(a_h), tm(decay)))
    y = jnp.moveaxis(y, 0, 1)
    mean = jnp.mean(y, axis=-1, keepdims=True)
    var = jnp.mean(jnp.square(y - mean), axis=-1, keepdims=True)
    y = ((y - mean) * lax.rsqrt(var + A_GN_EPS)).reshape(bsz, t, MIX_W) * ln_w + ln_b
    bonus = jnp.sum(r_h * k_h * r_k, axis=-1, keepdims=True) * v_h
    y = (y + bonus.reshape(bsz, t, MIX_W)) * g
    return y.astype(dt), p[:, -1:], s_fin


def gla_mix(p, s0, alpha_up, alpha_bias, norm_w):
    bsz, t, _ = p.shape
    dt = p.dtype
    chunk = min(CHUNK, t)
    n_c = t // chunk
    qk = B_HEADS * B_DK
    q, k, v, al, rg = split_cols(p, (qk, qk, MIX_W, B_GATE_RANK, MIX_W))
    log_a = jax.nn.log_sigmoid((al @ alpha_up + alpha_bias).astype(jnp.float32)) / B_GATE_TEMP

    def blocks(z, d):
        return z.astype(jnp.float32).reshape(bsz, n_c, chunk, B_HEADS, d).transpose(1, 0, 3, 2, 4)

    qb = blocks(q, B_DK) * B_DK ** -0.5
    kb, vb = blocks(k, B_DK), blocks(v, B_DV)
    cum = jnp.cumsum(blocks(log_a, B_DK), axis=3)
    causal = jnp.tril(jnp.ones((chunk, chunk), bool))

    def step(s, inp):
        q_c, k_c, v_c, b_c = inp
        inter = jnp.einsum('bhid,bhdv->bhiv', q_c * jnp.exp(b_c), s)
        rel = jnp.exp(jnp.where(causal[:, :, None],
                                b_c[:, :, :, None, :] - b_c[:, :, None, :, :], -jnp.inf))
        att = jnp.einsum('bhid,bhjd,bhijd->bhij', q_c, k_c, rel)
        intra = jnp.einsum('bhij,bhjv->bhiv', att, v_c)
        b_last = b_c[:, :, -1:, :]
        s = (jnp.exp(b_last[:, :, 0, :])[..., None] * s
             + jnp.einsum('bhjd,bhjv->bhdv', k_c * jnp.exp(b_last - b_c), v_c))
        return s, inter + intra

    s_fin, o = lax.scan(step, s0.astype(jnp.float32), (qb, kb, vb, cum))
    o = o.transpose(1, 0, 3, 2, 4).reshape(bsz, t, B_HEADS, B_DV)
    o = o * lax.rsqrt(jnp.mean(o * o, axis=-1, keepdims=True) + NORM_EPS)
    o = o.reshape(bsz, t, MIX_W) * norm_w * jax.nn.silu(rg.astype(jnp.float32))
    return o.astype(dt), s_fin


def band_attend(q, k, v, key_valid, n_past, rel_table):
    cq, lk = q.shape[1], k.shape[1]
    s = jnp.einsum('bihd,bjhd->bhij', q, k).astype(jnp.float32) * C_HD ** -0.5
    dist = jnp.arange(cq)[:, None] + n_past - jnp.arange(lk)[None, :]
    bias = rel_table[:, jnp.clip(dist, -C_REL_CLIP, C_REL_CLIP) + C_REL_CLIP]
    s = jnp.where(key_valid, s + bias.astype(jnp.float32), -jnp.inf)
    prob = jax.nn.softmax(s, axis=-1)
    return jnp.einsum('bhij,bjhd->bihd', prob.astype(v.dtype), v)


def band_mix_prompt(p, rel_table):
    bsz, t, _ = p.shape
    q, k, v = [z.reshape(bsz, t, C_HEADS, C_HD) for z in split_cols(p, (MIX_W, MIX_W, MIX_W))]
    band = C_BAND_CHUNKS * CHUNK
    pad = ((0, 0), (band, 0), (0, 0), (0, 0))
    kp, vp = jnp.pad(k, pad), jnp.pad(v, pad)

    def block(c):
        start = c * CHUNK
        qb = lax.dynamic_slice_in_dim(q, start, CHUNK, axis=1)
        kb = lax.dynamic_slice_in_dim(kp, start, band + CHUNK, axis=1)
        vb = lax.dynamic_slice_in_dim(vp, start, band + CHUNK, axis=1)
        valid = jnp.arange(band + CHUNK) >= band - start
        return band_attend(qb, kb, vb, valid, band, rel_table)

    o = lax.map(block, jnp.arange(t // CHUNK))
    o = o.transpose(1, 0, 2, 3, 4).reshape(bsz, t, MIX_W)
    keep = min(band, t)
    return o, k[:, t - keep:], v[:, t - keep:]


def band_mix_sample(p, cache_k, cache_v, rel_table):
    bsz, t, _ = p.shape
    q, k, v = [z.reshape(bsz, t, C_HEADS, C_HD) for z in split_cols(p, (MIX_W, MIX_W, MIX_W))]
    kb = jnp.concatenate([cache_k.astype(k.dtype), k], axis=1)
    vb = jnp.concatenate([cache_v.astype(v.dtype), v], axis=1)
    valid = jnp.ones((kb.shape[1],), bool)
    o = band_attend(q, kb, vb, valid, cache_k.shape[1], rel_table)
    return o.reshape(bsz, t, MIX_W), k, v


def deltanet_mix(p, conv_buf, s0, conv_w, a_log, dt_bias, norm_w):
    bsz, t, _ = p.shape
    dt = p.dtype
    chunk = min(CHUNK, t)
    n_c = t // chunk
    qkv_raw, beta_raw, a_raw, gate = split_cols(p, (3 * MIX_W, D_HEADS, D_HEADS, MIX_W))
    xc = jnp.concatenate([conv_buf.astype(dt), qkv_raw], axis=1)
    conv = xc[:, 0:t] * conv_w[0]
    for i in range(1, D_CONV):
        conv = conv + xc[:, i:i + t] * conv_w[i]
    qkv = jax.nn.silu(conv.astype(jnp.float32))
    q, k, v = [z.reshape(bsz, t, D_HEADS, D_HD) for z in split_cols(qkv, (MIX_W, MIX_W, MIX_W))]
    q = l2_normalize(q) * D_HD ** -0.5
    k = l2_normalize(k)
    beta = jax.nn.sigmoid(beta_raw.astype(jnp.float32))
    g = -jnp.exp(a_log.astype(jnp.float32)) * jax.nn.softplus(
        (a_raw + dt_bias).astype(jnp.float32))

    def blocks(z):
        return z.reshape(bsz, n_c, chunk, D_HEADS, z.shape[-1]).transpose(0, 3, 1, 2, 4)

    qb, kb, vb = blocks(q), blocks(k), blocks(v)
    beta_b = blocks(beta[..., None])
    cum = jnp.cumsum(blocks(g[..., None]), axis=3)
    diff = cum - jnp.swapaxes(cum, -1, -2)
    incl = jnp.tril(jnp.ones((chunk, chunk), bool))
    strict = jnp.tril(jnp.ones((chunk, chunk), bool), -1)
    decay = jnp.exp(jnp.where(incl, diff, -jnp.inf))
    kk = jnp.einsum('bhnid,bhnjd->bhnij', kb, kb)
    tri = jnp.eye(chunk, dtype=jnp.float32) + jnp.where(strict, beta_b * decay * kk, 0.0)
    u = lax.linalg.triangular_solve(tri, vb * beta_b, left_side=True, lower=True,
                                    unit_diagonal=True)
    w = lax.linalg.triangular_solve(tri, kb * (beta_b * jnp.exp(cum)), left_side=True,
                                    lower=True, unit_diagonal=True)
    att = jnp.einsum('bhnid,bhnjd->bhnij', qb, kb) * decay
    q_dec = qb * jnp.exp(cum)
    k_dec = kb * jnp.exp(cum[:, :, :, -1:] - cum)
    g_last = jnp.exp(cum[:, :, :, -1])

    def step(s, inp):
        u_c, w_c, att_c, qd_c, kd_c, gl_c = inp
        delta = u_c - jnp.einsum('bhik,bhkv->bhiv', w_c, s)
        o_c = (jnp.einsum('bhik,bhkv->bhiv', qd_c, s)
               + jnp.einsum('bhij,bhjv->bhiv', att_c, delta))
        s = gl_c[..., None] * s + jnp.einsum('bhjk,bhjv->bhkv', kd_c, delta)
        return s, o_c

    def sw(z):
        return jnp.moveaxis(z, 2, 0)

    s_fin, o = lax.scan(step, s0.astype(jnp.float32),
                        (sw(u), sw(w), sw(att), sw(q_dec), sw(k_dec), sw(g_last)))
    o = jnp.moveaxis(o, 0, 2).transpose(0, 2, 3, 1, 4).reshape(bsz, t, D_HEADS, D_HD)
    o = o * lax.rsqrt(jnp.mean(o * o, axis=-1, keepdims=True) + NORM_EPS) * norm_w
    o = o.reshape(bsz, t, MIX_W) * jax.nn.silu(gate.astype(jnp.float32))
    return o.astype(dt), xc[:, t:], s_fin


def layer_forward(x, shift_prev, s_a, s_b, band_k, band_v, conv_buf, s_d, lw):
    u = rms_norm(x, lw['norm_mix_pre'])
    p_a, p_b, p_c, p_d = split_cols(u @ lw['w_in'], (A_IN, B_IN, C_IN, D_IN))
    y_a, shift_new, s_a_new = rwkv7_mix(p_a, shift_prev, s_a, lw['a_mu'], lw['a_w0'], lw['a_w_up'],
                                        lw['a_a0'], lw['a_a_up'], lw['a_g_up'], lw['a_k_k'],
                                        lw['a_k_a'], lw['a_r_k'], lw['a_ln_w'], lw['a_ln_b'])
    y_b, s_b_new = gla_mix(p_b, s_b, lw['b_alpha_up'], lw['b_alpha_bias'], lw['b_norm'])
    if band_k is None:
        y_c, k_new, v_new = band_mix_prompt(p_c, lw['c_rel_bias'])
    else:
        y_c, k_new, v_new = band_mix_sample(p_c, band_k, band_v, lw['c_rel_bias'])
    y_d, conv_new, s_d_new = deltanet_mix(p_d, conv_buf, s_d, lw['d_conv_w'], lw['d_a_log'],
                                          lw['d_dt_bias'], lw['d_norm'])
    merged = None
    for i, y_br in enumerate((y_a, y_b, y_c, y_d)):
        term = jax.nn.sigmoid(u @ lw['w_merge_gate'][i]) * (y_br @ lw['w_branch'][i])
        merged = term if merged is None else merged + term
    h = x + rms_norm(merged @ lw['w_out'], lw['norm_mix_post'])
    z = rms_norm(h, lw['norm_ffn_pre'])
    f = jnp.square(jax.nn.relu(z @ lw['w_ffn_up'])) @ lw['w_ffn_down']
    out = h + rms_norm(f, lw['norm_ffn_post'])
    return out, (shift_new, s_a_new, s_b_new, k_new, v_new, conv_new, s_d_new)


def stack_states(states, i):
    return jnp.stack([s[i] for s in states])


def setup_inputs(seed: int = 0) -> dict:
    key = jax.random.key(seed)
    keys = iter(jax.random.split(key, 48))
    f32 = jnp.float32

    def normal(shape, scale):
        return scale * jax.random.normal(next(keys), shape, f32)

    def gain(shape, base=1.0):
        return base + 0.05 * jax.random.normal(next(keys), shape, f32)

    def uniform(shape, lo, hi):
        return jax.random.uniform(next(keys), shape, f32, lo, hi)

    band_past = min(C_BAND_CHUNKS * CHUNK, PAST_LEN)
    return {
        'x_prompt': normal((BATCH, SEQ, D_MODEL), 1.0),
        'x_sample': normal((DEC_BATCH, DEC_SEQ, D_MODEL), 1.0),
        'state_rwkv_shift': normal((DEPTH, DEC_BATCH, 1, A_IN), 1.0),
        'state_rwkv': normal((DEPTH, DEC_BATCH, A_HEADS, A_HD, A_HD), 0.5),
        'state_gla': normal((DEPTH, DEC_BATCH, B_HEADS, B_DK, B_DV), 0.5),
        'cache_band_k': normal((DEPTH, DEC_BATCH, band_past, C_HEADS, C_HD), 1.0),
        'cache_band_v': normal((DEPTH, DEC_BATCH, band_past, C_HEADS, C_HD), 1.0),
        'state_dn_conv': normal((DEPTH, DEC_BATCH, D_CONV - 1, 3 * MIX_W), 1.0),
        'state_dn': normal((DEPTH, DEC_BATCH, D_HEADS, D_HD, D_HD), 0.3),
        'norm_mix_pre': gain((DEPTH, D_MODEL)),
        'norm_mix_post': gain((DEPTH, D_MODEL)),
        'norm_ffn_pre': gain((DEPTH, D_MODEL)),
        'norm_ffn_post': gain((DEPTH, D_MODEL)),
        'w_in': normal((DEPTH, D_MODEL, IN_TOTAL), D_MODEL ** -0.5),
        'w_merge_gate': normal((DEPTH, N_BRANCH, D_MODEL, D_MODEL), D_MODEL ** -0.5),
        'w_branch': normal((DEPTH, N_BRANCH, MIX_W, D_MODEL), MIX_W ** -0.5),
        'w_out': normal((DEPTH, D_MODEL, D_MODEL), D_MODEL ** -0.5),
        'w_ffn_up': normal((DEPTH, D_MODEL, D_FF), D_MODEL ** -0.5),
        'w_ffn_down': normal((DEPTH, D_FF, D_MODEL), D_FF ** -0.5),
        'a_mu': uniform((DEPTH, A_IN), 0.0, 1.0),
        'a_w0': normal((DEPTH, MIX_W), 0.5),
        'a_w_up': normal((DEPTH, A_DECAY_RANK, MIX_W), A_DECAY_RANK ** -0.5),
        'a_a0': normal((DEPTH, MIX_W), 0.3),
        'a_a_up': normal((DEPTH, A_ICL_RANK, MIX_W), A_ICL_RANK ** -0.5),
        'a_g_up': normal((DEPTH, A_GATE_RANK, MIX_W), A_GATE_RANK ** -0.5),
        'a_k_k': gain((DEPTH, MIX_W), 0.85),
        'a_k_a': gain((DEPTH, MIX_W)),
        'a_r_k': normal((DEPTH, A_HEADS, A_HD), 0.3),
        'a_ln_w': gain((DEPTH, MIX_W)),
        'a_ln_b': normal((DEPTH, MIX_W), 0.02),
        'b_alpha_up': normal((DEPTH, B_GATE_RANK, B_HEADS * B_DK), B_GATE_RANK ** -0.5),
        'b_alpha_bias': normal((DEPTH, B_HEADS * B_DK), 0.5),
        'b_norm': gain((DEPTH, MIX_W)),
        'c_rel_bias': normal((DEPTH, C_HEADS, 2 * C_REL_CLIP + 1), 0.5),
        'd_conv_w': normal((DEPTH, D_CONV, 3 * MIX_W), D_CONV ** -0.5),
        'd_a_log': jnp.log(uniform((DEPTH, D_HEADS), 1.0, 16.0)),
        'd_dt_bias': uniform((DEPTH, D_HEADS), -6.0, -2.0),
        'd_norm': gain((DEPTH, D_HD)),
    }


def reference(x_prompt, x_sample, state_rwkv_shift, state_rwkv, state_gla, cache_band_k,
              cache_band_v, state_dn_conv, state_dn, norm_mix_pre, norm_mix_post, norm_ffn_pre,
              norm_ffn_post, w_in, w_merge_gate, w_branch, w_out, w_ffn_up, w_ffn_down, a_mu,
              a_w0, a_w_up, a_a0, a_a_up, a_g_up, a_k_k, a_k_a, a_r_k, a_ln_w, a_ln_b,
              b_alpha_up, b_alpha_bias, b_norm, c_rel_bias, d_conv_w, d_a_log, d_dt_bias, d_norm):
    bsz = x_prompt.shape[0]
    f32 = jnp.float32
    y_p, y_s = x_prompt, x_sample
    new_p, new_s = [], []
    for l in range(DEPTH):
        lw = {
            'norm_mix_pre': norm_mix_pre[l], 'norm_mix_post': norm_mix_post[l],
            'norm_ffn_pre': norm_ffn_pre[l], 'norm_ffn_post': norm_ffn_post[l],
            'w_in': w_in[l], 'w_merge_gate': w_merge_gate[l], 'w_branch': w_branch[l],
            'w_out': w_out[l], 'w_ffn_up': w_ffn_up[l], 'w_ffn_down': w_ffn_down[l],
            'a_mu': a_mu[l], 'a_w0': a_w0[l], 'a_w_up': a_w_up[l], 'a_a0': a_a0[l],
            'a_a_up': a_a_up[l], 'a_g_up': a_g_up[l], 'a_k_k': a_k_k[l], 'a_k_a': a_k_a[l],
            'a_r_k': a_r_k[l], 'a_ln_w': a_ln_w[l], 'a_ln_b': a_ln_b[l],
            'b_alpha_up': b_alpha_up[l], 'b_alpha_bias': b_alpha_bias[l], 'b_norm': b_norm[l],
            'c_rel_bias': c_rel_bias[l], 'd_conv_w': d_conv_w[l], 'd_a_log': d_a_log[l],
            'd_dt_bias': d_dt_bias[l], 'd_norm': d_norm[l],
        }
        y_p, st_p = layer_forward(
            y_p,
            jnp.zeros((bsz, 1, A_IN), y_p.dtype),
            jnp.zeros((bsz, A_HEADS, A_HD, A_HD), f32),
            jnp.zeros((bsz, B_HEADS, B_DK, B_DV), f32),
            None, None,
            jnp.zeros((bsz, D_CONV - 1, 3 * MIX_W), y_p.dtype),
            jnp.zeros((bsz, D_HEADS, D_HD, D_HD), f32),
            lw)
        y_s, st_s = layer_forward(y_s, state_rwkv_shift[l], state_rwkv[l], state_gla[l],
                                  cache_band_k[l], cache_band_v[l], state_dn_conv[l],
                                  state_dn[l], lw)
        new_p.append(st_p)
        new_s.append(st_s)
    return (y_p, y_s,
            stack_states(new_p, 0), stack_states(new_p, 1), stack_states(new_p, 2),
            stack_states(new_p, 3), stack_states(new_p, 4), stack_states(new_p, 5),
            stack_states(new_p, 6),
            stack_states(new_s, 0), stack_states(new_s, 1), stack_states(new_s, 2),
            stack_states(new_s, 3), stack_states(new_s, 4), stack_states(new_s, 5),
            stack_states(new_s, 6))
```

```python
import functools

import numpy as np
import jax
import jax.numpy as jnp
from jax import lax
from jax.experimental import pallas as pl
from jax.experimental.pallas import tpu as pltpu

F32 = jnp.float32
BF16 = jnp.bfloat16

D_MODEL = 2048
MIX_W = 512
D_FF = 4 * D_MODEL
CHUNK = 64
LANES = 128
NORM_EPS = 1e-6
VMEM_LIMIT = 56 * 1024 * 1024

A_HEADS, A_HD = 8, 64
A_DECAY_SCALE = 0.6065306597
A_GN_EPS = 64e-5
A_IN = 1792
B_HEADS, B_DK, B_DV = 4, 64, 128
B_GATE_TEMP = 16.0
B_IN = 1552
C_HEADS, C_HD = 8, 64
C_BAND = 8 * CHUNK
C_REL_CLIP = 2 * CHUNK
C_WIN = C_BAND + 2 * CHUNK
C_IN = 1536
D_HEADS, D_HD = 4, 128
D_CONV = 4
D_IN = 2056
IN_TOTAL = A_IN + B_IN + C_IN + D_IN


def _head_pad(cols, head):
    out = []
    for h in range(len(cols) // head):
        out.extend(cols[h * head:(h + 1) * head])
        out.extend([-1] * (LANES - head))
    return out


def _pad_to(cols, width):
    return list(cols) + [-1] * (width - len(cols))


def _build_layout():
    a0, b0, c0, d0 = 0, A_IN, A_IN + B_IN, A_IN + B_IN + C_IN
    rng = lambda s, n: list(range(s, s + n))
    segs = [
        ("a_r", _head_pad(rng(a0, 512), 64)), ("a_k", _head_pad(rng(a0 + 512, 512), 64)),
        ("a_v", _head_pad(rng(a0 + 1024, 512), 64)),
        ("c_q", _head_pad(rng(c0, 512), 64)), ("c_k", _head_pad(rng(c0 + 512, 512), 64)),
        ("c_v", _head_pad(rng(c0 + 1024, 512), 64)),
        ("d_qkv", rng(d0, 1536)), ("d_gate", rng(d0 + 1544, 512)),
        ("b_q", _head_pad(rng(b0, 256), 64)), ("b_k", _head_pad(rng(b0 + 256, 256), 64)),
        ("b_v", rng(b0 + 512, 512)), ("b_rg", rng(b0 + 1040, 512)),
        ("a_wa", rng(a0 + 1536, 128)), ("a_gl", rng(a0 + 1664, 128)),
        ("b_al", _pad_to(rng(b0 + 1024, 16), LANES)), ("d_ba", _pad_to(rng(d0 + 1536, 8), LANES)),
    ]
    idx, off = [], {}
    for name, cols in segs:
        off[name] = len(idx)
        idx.extend(cols)
    return np.asarray(idx, np.int32), off


_P_IDX, _P_OFF = _build_layout()
P_COLS = int(_P_IDX.shape[0])
_A_SLOT_IDX = np.asarray(_head_pad(list(range(512)), 64), np.int32)
_B_SLOT_IDX = np.asarray(_head_pad(list(range(256)), 64), np.int32)
_A_ROW_IDX = np.concatenate([_A_SLOT_IDX, np.where(_A_SLOT_IDX >= 0, _A_SLOT_IDX + 512, -1),
                             np.where(_A_SLOT_IDX >= 0, _A_SLOT_IDX + 1024, -1),
                             np.arange(1536, 1792, dtype=np.int32)]).astype(np.int32)
A_ROW = int(_A_ROW_IDX.shape[0])


def _take_cols(x, idx):
    zero = jnp.zeros(x.shape[:-1] + (1,), x.dtype)
    safe = np.where(idx >= 0, idx, x.shape[-1])
    return jnp.take(jnp.concatenate([x, zero], axis=-1), jnp.asarray(safe), axis=-1)


def _take_rows(x, idx):
    return jnp.swapaxes(_take_cols(jnp.swapaxes(x, -1, -2), idx), -1, -2)


def _a_row_to_orig():
    pos = np.zeros(A_IN, np.int32)
    base = {0: _P_OFF["a_r"], 1: _P_OFF["a_k"], 2: _P_OFF["a_v"]}
    for i in range(1536):
        part, j = divmod(i, 512)
        pos[i] = base[part] + (j // 64) * LANES + (j % 64)
    pos[1536:1664] = _P_OFF["a_wa"] + np.arange(128)
    pos[1664:1792] = _P_OFF["a_gl"] + np.arange(128)
    return pos


_A_ORIG_POS = _a_row_to_orig()


def _mm(a, b):
    return jnp.dot(a.astype(BF16), b.astype(BF16), preferred_element_type=F32)


def _mm_nt(a, b):
    return lax.dot_general(a.astype(BF16), b.astype(BF16), (((1,), (1,)), ((), ())),
                           preferred_element_type=F32)


def _split(x):
    hi = x.astype(BF16)
    lo = (x - hi.astype(F32)).astype(BF16)
    return hi, lo


def _mm3(a, b):
    ah, al = _split(a)
    bh, bl = _split(b)
    return (jnp.dot(ah, bh, preferred_element_type=F32) + jnp.dot(ah, bl, preferred_element_type=F32)
            + jnp.dot(al, bh, preferred_element_type=F32))


def _mm_exact_lhs(a_bf16, b):
    bh, bl = _split(b)
    return jnp.dot(a_bf16, bh, preferred_element_type=F32) + jnp.dot(a_bf16, bl, preferred_element_type=F32)


def _iota2(shape, axis):
    return lax.broadcasted_iota(jnp.int32, shape, axis)


def _tri_inverse(a):
    n = a.shape[0]
    eye = (_iota2((n, n), 0) == _iota2((n, n), 1)).astype(F32)
    t = eye + a
    x = a
    for _ in range(int(np.log2(n)) - 1):
        x = _mm3(x, x)
        t = t + _mm3(t, x)
    return t


def _sigmoid(x):
    return 1.0 / (1.0 + jnp.exp(-x))


def _softplus(x):
    return jnp.maximum(x, 0.0) + jnp.log(1.0 + jnp.exp(-jnp.abs(x)))


def _rms(x, gain):
    return x * lax.rsqrt(jnp.mean(x * x, axis=-1, keepdims=True) + NORM_EPS) * gain


def _lower_tri_bf16(n):
    return (_iota2((n, n), 0) >= _iota2((n, n), 1)).astype(BF16)


def _shift_rows(x, s, carry_rows):
    rows = _iota2(x.shape, 0)
    out = pltpu.roll(x, s, 0)
    for t in range(s):
        out = jnp.where(rows == t, jnp.broadcast_to(carry_rows[t], x.shape), out)
    return out


def _inproj_kernel(x_ref, g_ref, w_ref, u_ref, p_ref, u_scr):
    @pl.when(pl.program_id(1) == 0)
    def _():
        u = _rms(x_ref[...], g_ref[...]).astype(BF16)
        u_scr[...] = u
        u_ref[...] = u

    p_ref[...] = jnp.dot(u_scr[...], w_ref[...], preferred_element_type=F32)


def _inproj(x2, gain, w):
    n = x2.shape[0]
    tm = min(n, 1024)
    tn = 512
    return pl.pallas_call(
        _inproj_kernel,
        grid=(n // tm, P_COLS // tn),
        in_specs=[pl.BlockSpec((tm, D_MODEL), lambda i, j: (i, 0)),
                  pl.BlockSpec((1, D_MODEL), lambda i, j: (0, 0)),
                  pl.BlockSpec((D_MODEL, tn), lambda i, j: (0, j))],
        out_specs=[pl.BlockSpec((tm, D_MODEL), lambda i, j: (i, 0)),
                   pl.BlockSpec((tm, tn), lambda i, j: (i, j))],
        out_shape=[jax.ShapeDtypeStruct((n, D_MODEL), BF16), jax.ShapeDtypeStruct((n, P_COLS), F32)],
        scratch_shapes=[pltpu.VMEM((tm, D_MODEL), BF16)],
        compiler_params=pltpu.CompilerParams(dimension_semantics=("parallel", "arbitrary"),
                                             vmem_limit_bytes=VMEM_LIMIT),
        name="inproj",
    )(x2, gain, w)


def _rwkv_kernel(pr_ref, pk_ref, pv_ref, pwa_ref, pgl_ref, sh0_ref, s0_ref, mu_ref, w0_ref, wup_ref,
                 a0_ref, aup_ref, gup_ref, kk_ref, ka_ref, rk_ref, lnw_ref, lnb_ref,
                 y_ref, sout_ref, carry_scr, s_scr):
    c = pl.program_id(1)
    C = CHUNK
    HW = A_HEADS * LANES

    @pl.when(c == 0)
    def _():
        carry_scr[...] = sh0_ref[...]
        s_scr[...] = s0_ref[...]

    def token_shift(p_ref, lo, hi):
        p = p_ref[...]
        prev = _shift_rows(p, 1, [carry_scr[:, lo:hi]])
        carry_scr[:, lo:hi] = p[C - 1:C, :]
        return p + (prev - p) * mu_ref[:, lo:hi]

    xr = token_shift(pr_ref, 0, HW)
    xk = token_shift(pk_ref, HW, 2 * HW)
    xv = token_shift(pv_ref, 2 * HW, 3 * HW)
    xwa = token_shift(pwa_ref, 3 * HW, 3 * HW + LANES)
    xgl = token_shift(pgl_ref, 3 * HW + LANES, 3 * HW + 2 * LANES)

    logw = -A_DECAY_SCALE * _sigmoid(w0_ref[...] + _mm(jnp.tanh(xwa), wup_ref[...]))
    a = _sigmoid(a0_ref[...] + _mm(xwa, aup_ref[...]))
    g = _mm(_sigmoid(xgl), gup_ref[...])
    kkv = xk * kk_ref[...]
    kmod = xk * (1.0 + (a - 1.0) * ka_ref[...])
    cum = _mm_exact_lhs(_lower_tri_bf16(C), logw)

    r2 = _iota2((2 * C, 2 * C), 0)
    c2 = _iota2((2 * C, 2 * C), 1)
    ti, tj = r2 % C, c2 % C
    mask2 = jnp.logical_or(ti > tj, jnp.logical_and(r2 >= C, ti == tj))
    real = _iota2((C, LANES), 1) < A_HD

    for h in range(A_HEADS):
        sl = slice(h * LANES, (h + 1) * LANES)
        kk_h = kkv[:, sl]
        kk_h = kk_h * lax.rsqrt(jnp.sum(kk_h * kk_h, axis=-1, keepdims=True) + 1e-6)
        cum_h, lw_h = cum[:, sl], logw[:, sl]
        g_incl = jnp.exp(cum_h)
        g_inv = jnp.exp(-cum_h)
        g_last = g_incl[C - 1:C, :]
        r_h, v_h, k_h = xr[:, sl], xv[:, sl], kmod[:, sl]
        at = kk_h * jnp.exp(cum_h - lw_h)
        rt = r_h * g_incl
        bh = -(kk_h * a[:, sl]) * g_inv
        kh = k_h * g_inv
        lhs2 = jnp.concatenate([at, rt], axis=0).astype(BF16)
        rhs2 = jnp.concatenate([bh, kh], axis=0).astype(BF16)
        aa = jnp.where(mask2, _mm_nt(lhs2, rhs2), 0.0)
        s_old = s_scr[h]
        p0 = _mm_nt(lhs2, s_old)
        a_a, a_r = aa[0:C], aa[C:2 * C]
        t_inv = _tri_inverse(a_a[:, 0:C])
        rhs_u = p0[0:C] + _mm(a_a, jnp.concatenate([jnp.zeros_like(v_h), v_h], axis=0))
        u = _mm(t_inv, rhs_u)
        z = jnp.concatenate([u, v_h], axis=0)
        y = p0[C:2 * C] + _mm(a_r, z)
        bk = jnp.concatenate([bh * g_last, kh * g_last], axis=0)
        s_scr[h] = s_old * g_last + _mm(z.T, bk)

        mean = jnp.sum(y, axis=-1, keepdims=True) * (1.0 / A_HD)
        d = jnp.where(real, y - mean, 0.0)
        var = jnp.sum(d * d, axis=-1, keepdims=True) * (1.0 / A_HD)
        yn = d * lax.rsqrt(var + A_GN_EPS) * lnw_ref[:, sl] + lnb_ref[:, sl]
        bonus = jnp.sum(r_h * k_h * rk_ref[:, sl], axis=-1, keepdims=True) * v_h
        y_ref[:, sl] = ((yn + bonus) * g[:, sl]).astype(BF16)

    @pl.when(c == pl.num_programs(1) - 1)
    def _():
        sout_ref[...] = s_scr[...]


def _rwkv(p3, shift_prev, s0, lw):
    bsz, t, _ = p3.shape
    hw = A_HEADS * LANES
    blk = lambda w, j: pl.BlockSpec((None, CHUNK, w), lambda b, c: (b, c, j))
    full = lambda shp: pl.BlockSpec(shp, lambda b, c: (0,) * len(shp))
    return pl.pallas_call(
        _rwkv_kernel,
        grid=(bsz, t // CHUNK),
        in_specs=[blk(hw, _P_OFF["a_r"] // hw), blk(hw, _P_OFF["a_k"] // hw), blk(hw, _P_OFF["a_v"] // hw),
                  blk(LANES, _P_OFF["a_wa"] // LANES), blk(LANES, _P_OFF["a_gl"] // LANES),
                  pl.BlockSpec((None, 1, A_ROW), lambda b, c: (b, 0, 0)),
                  pl.BlockSpec((None, A_HEADS, LANES, LANES), lambda b, c: (b, 0, 0, 0)),
                  full((1, A_ROW)), full((1, hw)), full((LANES, hw)), full((1, hw)), full((LANES, hw)),
                  full((LANES, hw)), full((1, hw)), full((1, hw)), full((1, hw)), full((1, hw)), full((1, hw))],
        out_specs=[pl.BlockSpec((None, CHUNK, hw), lambda b, c: (b, c, 0)),
                   pl.BlockSpec((None, A_HEADS, LANES, LANES), lambda b, c: (b, 0, 0, 0))],
        out_shape=[jax.ShapeDtypeStruct((bsz, t, hw), BF16),
                   jax.ShapeDtypeStruct((bsz, A_HEADS, LANES, LANES), F32)],
        scratch_shapes=[pltpu.VMEM((1, A_ROW), F32), pltpu.VMEM((A_HEADS, LANES, LANES), F32)],
        compiler_params=pltpu.CompilerParams(dimension_semantics=("parallel", "arbitrary"),
                                             vmem_limit_bytes=VMEM_LIMIT),
        name="rwkv7",
    )(p3, p3, p3, p3, p3, shift_prev, s0, lw["a_mu"], lw["a_w0"], lw["a_w_up"], lw["a_a0"], lw["a_a_up"],
      lw["a_g_up"], lw["a_k_k"], lw["a_k_a"], lw["a_r_k"], lw["a_ln_w"], lw["a_ln_b"])


def _gla_kernel(q_ref, k_ref, v_ref, rg_ref, al_ref, s0_ref, aup_ref, ab_ref, nw_ref,
                y_ref, sout_ref, s_scr):
    c = pl.program_id(1)
    C = CHUNK

    @pl.when(c == 0)
    def _():
        s_scr[...] = s0_ref[...]

    x = _mm(al_ref[...], aup_ref[...]) + ab_ref[...]
    loga = -_softplus(-x) * (1.0 / B_GATE_TEMP)
    cum = _mm_exact_lhs(_lower_tri_bf16(C), loga)
    causal = _iota2((C, C), 0) >= _iota2((C, C), 1)
    q, k, v, rg = q_ref[...], k_ref[...], v_ref[...], rg_ref[...]

    for h in range(B_HEADS):
        sl = slice(h * LANES, (h + 1) * LANES)
        b = cum[:, sl]
        b_last = b[C - 1:C, :]
        qt = q[:, sl] * (B_DK ** -0.5) * jnp.exp(b)
        kt = k[:, sl] * jnp.exp(-b)
        v_h = v[:, sl]
        att = jnp.where(causal, _mm_nt(qt, kt), 0.0)
        s_old = s_scr[h]
        o = _mm_nt(qt, s_old) + _mm(att, v_h)
        kd = k[:, sl] * jnp.exp(b_last - b)
        zeros = jnp.zeros_like(v_h)
        vt = jnp.concatenate([v_h, zeros], axis=0).T
        s_scr[h] = s_old * jnp.exp(b_last) + _mm(vt, jnp.concatenate([kd, zeros], axis=0))
        o = o * lax.rsqrt(jnp.mean(o * o, axis=-1, keepdims=True) + NORM_EPS)
        rg_h = rg[:, sl]
        y_ref[:, sl] = (o * nw_ref[:, sl] * (rg_h * _sigmoid(rg_h))).astype(BF16)

    @pl.when(c == pl.num_programs(1) - 1)
    def _():
        sout_ref[...] = s_scr[...]


def _gla(p3, s0, lw):
    bsz, t, _ = p3.shape
    blk = lambda w, j: pl.BlockSpec((None, CHUNK, w), lambda b, c: (b, c, j))
    full = lambda shp: pl.BlockSpec(shp, lambda b, c: (0,) * len(shp))
    st = pl.BlockSpec((None, B_HEADS, LANES, LANES), lambda b, c: (b, 0, 0, 0))
    return pl.pallas_call(
        _gla_kernel,
        grid=(bsz, t // CHUNK),
        in_specs=[blk(MIX_W, _P_OFF["b_q"] // MIX_W), blk(MIX_W, _P_OFF["b_k"] // MIX_W),
                  blk(MIX_W, _P_OFF["b_v"] // MIX_W), blk(MIX_W, _P_OFF["b_rg"] // MIX_W),
                  blk(LANES, _P_OFF["b_al"] // LANES), st,
                  full((LANES, MIX_W)), full((1, MIX_W)), full((1, MIX_W))],
        out_specs=[pl.BlockSpec((None, CHUNK, MIX_W), lambda b, c: (b, c, 0)), st],
        out_shape=[jax.ShapeDtypeStruct((bsz, t, MIX_W), BF16),
                   jax.ShapeDtypeStruct((bsz, B_HEADS, LANES, LANES), F32)],
        scratch_shapes=[pltpu.VMEM((B_HEADS, LANES, LANES), F32)],
        compiler_params=pltpu.CompilerParams(dimension_semantics=("parallel", "arbitrary"),
                                             vmem_limit_bytes=VMEM_LIMIT),
        name="gla",
    )(p3, p3, p3, p3, p3, s0, lw["b_alpha_up"], lw["b_alpha_bias"], lw["b_norm"])


def _band_kernel(q_ref, k_ref, v_ref, g_ref, y_ref, bias_scr, *, pad_rows):
    b = pl.program_id(0)
    c = pl.program_id(1)
    C = CHUNK

    @pl.when(jnp.logical_and(b == 0, c == 0))
    def _():
        rows = _iota2((C, C_WIN), 0)
        for h in range(C_HEADS):
            x = jnp.broadcast_to(g_ref[h:h + 1, :], (C, C_WIN))
            for bit in range(6):
                x = jnp.where(((rows >> bit) & 1) == 1, pltpu.roll(x, 1 << bit, 1), x)
            bias_scr[h] = x

    start = pl.multiple_of(c * C, C)
    col = _iota2((C, C_WIN), 1)
    valid = jnp.logical_and(col < C_BAND + C, col + c * C >= pad_rows)
    for h in range(C_HEADS):
        sl = slice(h * LANES, (h + 1) * LANES)
        qh = (q_ref[:, sl] * (C_HD ** -0.5)).astype(BF16)
        s = _mm_nt(qh, k_ref[pl.ds(start, C_WIN), sl]) + bias_scr[h]
        s = jnp.where(valid, s, -jnp.inf)
        p = jnp.exp(s - jnp.max(s, axis=-1, keepdims=True))
        denom = jnp.sum(p, axis=-1, keepdims=True)
        o = _mm(p, v_ref[pl.ds(start, C_WIN), sl]) / denom
        y_ref[:, sl] = o.astype(BF16)


def _band(p3, kpad, vpad, gtab, pad_rows):
    bsz, t, _ = p3.shape
    hw = C_HEADS * LANES
    tp = kpad.shape[1]
    kv = pl.BlockSpec((None, tp, hw), lambda b, c: (b, 0, 0))
    return pl.pallas_call(
        functools.partial(_band_kernel, pad_rows=pad_rows),
        grid=(bsz, t // CHUNK),
        in_specs=[pl.BlockSpec((None, CHUNK, hw), lambda b, c: (b, c, _P_OFF["c_q"] // hw)), kv, kv,
                  pl.BlockSpec((C_HEADS, C_WIN), lambda b, c: (0, 0))],
        out_specs=pl.BlockSpec((None, CHUNK, hw), lambda b, c: (b, c, 0)),
        out_shape=jax.ShapeDtypeStruct((bsz, t, hw), BF16),
        scratch_shapes=[pltpu.VMEM((C_HEADS, CHUNK, C_WIN), F32)],
        compiler_params=pltpu.CompilerParams(dimension_semantics=("arbitrary", "arbitrary"),
                                             vmem_limit_bytes=VMEM_LIMIT),
        name="band_attn",
    )(p3, kpad, vpad, gtab)


def _dn_kernel(qkv_ref, ba_ref, gate_ref, cb0_ref, s0_ref, cw_ref, alog_ref, dtb_ref, nw_ref,
               y_ref, sout_ref, carry_scr, s_scr):
    c = pl.program_id(1)
    C = CHUNK

    @pl.when(c == 0)
    def _():
        carry_scr[...] = cb0_ref[...]
        s_scr[...] = s0_ref[...]

    x = qkv_ref[...]
    c0, c1, c2 = carry_scr[5:6, :], carry_scr[6:7, :], carry_scr[7:8, :]
    conv = (_shift_rows(x, 3, [c0, c1, c2]) * cw_ref[0:1, :] + _shift_rows(x, 2, [c1, c2]) * cw_ref[1:2, :]
            + _shift_rows(x, 1, [c2]) * cw_ref[2:3, :] + x * cw_ref[3:4, :])
    carry_scr[...] = x[C - 8:C, :]
    qkv = conv * _sigmoid(conv)

    ba = ba_ref[...]
    beta_all = _sigmoid(ba)
    g_all = -jnp.exp(alog_ref[...]) * _softplus(ba + dtb_ref[...])
    tri = _lower_tri_bf16(C)
    cum_all = _mm_exact_lhs(tri, g_all)
    lane = _iota2((C, LANES), 1)
    col_of = lambda arr, j: jnp.sum(jnp.where(lane == j, arr, 0.0), axis=-1, keepdims=True)
    ri, ci = _iota2((C, C), 0), _iota2((C, C), 1)
    after = (ri > ci).astype(F32)
    gate = gate_ref[...]

    for h in range(D_HEADS):
        sl = slice(h * LANES, (h + 1) * LANES)
        beta = col_of(beta_all, h)
        g_c = col_of(g_all, D_HEADS + h)
        cum = col_of(cum_all, D_HEADS + h)
        diff = _mm_exact_lhs(tri, g_c * after)
        decay = jnp.where(ri >= ci, jnp.exp(diff), 0.0)
        q_h = qkv[:, sl]
        k_h = qkv[:, MIX_W + h * LANES:MIX_W + (h + 1) * LANES]
        v_h = qkv[:, 2 * MIX_W + h * LANES:2 * MIX_W + (h + 1) * LANES]
        q_h = q_h * lax.rsqrt(jnp.sum(q_h * q_h, axis=-1, keepdims=True) + 1e-6) * (D_HD ** -0.5)
        k_h = k_h * lax.rsqrt(jnp.sum(k_h * k_h, axis=-1, keepdims=True) + 1e-6)
        kk = _mm_nt(k_h, k_h)
        a_mat = jnp.where(ri > ci, beta * decay * kk, 0.0)
        t_inv = _tri_inverse(-a_mat)
        ecum = jnp.exp(cum)
        uw = _mm(t_inv, jnp.concatenate([v_h * beta, k_h * (beta * ecum)], axis=1))
        u, w = uw[:, 0:LANES], uw[:, LANES:2 * LANES]
        att = _mm_nt(q_h, k_h) * decay
        s_old = s_scr[h]
        delta = u - _mm(w, s_old)
        o = _mm(q_h * ecum, s_old) + _mm(att, delta)
        cum_last = cum[C - 1:C, :]
        kd = k_h * jnp.exp(cum_last - cum)
        zeros = jnp.zeros_like(kd)
        kdt = jnp.concatenate([kd, zeros], axis=0).T
        s_scr[h] = s_old * jnp.exp(cum_last) + _mm(kdt, jnp.concatenate([delta, zeros], axis=0))
        o = o * lax.rsqrt(jnp.mean(o * o, axis=-1, keepdims=True) + NORM_EPS) * nw_ref[...]
        gt = gate[:, sl]
        y_ref[:, sl] = (o * (gt * _sigmoid(gt))).astype(BF16)

    @pl.when(c == pl.num_programs(1) - 1)
    def _():
        sout_ref[...] = s_scr[...]


def _dn(p3, conv_buf, s0, lw):
    bsz, t, _ = p3.shape
    w3 = 3 * MIX_W
    full = lambda shp: pl.BlockSpec(shp, lambda b, c: (0,) * len(shp))
    st = pl.BlockSpec((None, D_HEADS, LANES, LANES), lambda b, c: (b, 0, 0, 0))
    return pl.pallas_call(
        _dn_kernel,
        grid=(bsz, t // CHUNK),
        in_specs=[pl.BlockSpec((None, CHUNK, w3), lambda b, c: (b, c, _P_OFF["d_qkv"] // w3)),
                  pl.BlockSpec((None, CHUNK, LANES), lambda b, c: (b, c, _P_OFF["d_ba"] // LANES)),
                  pl.BlockSpec((None, CHUNK, MIX_W), lambda b, c: (b, c, _P_OFF["d_gate"] // MIX_W)),
                  pl.BlockSpec((None, 8, w3), lambda b, c: (b, 0, 0)), st,
                  full((D_CONV, w3)), full((1, LANES)), full((1, LANES)), full((1, LANES))],
        out_specs=[pl.BlockSpec((None, CHUNK, MIX_W), lambda b, c: (b, c, 0)), st],
        out_shape=[jax.ShapeDtypeStruct((bsz, t, MIX_W), BF16),
                   jax.ShapeDtypeStruct((bsz, D_HEADS, LANES, LANES), F32)],
        scratch_shapes=[pltpu.VMEM((8, w3), F32), pltpu.VMEM((D_HEADS, LANES, LANES), F32)],
        compiler_params=pltpu.CompilerParams(dimension_semantics=("parallel", "arbitrary"),
                                             vmem_limit_bytes=VMEM_LIMIT),
        name="deltanet",
    )(p3, p3, p3, conv_buf, s0, lw["d_conv_w"], lw["d_a_log"], lw["d_dt_bias"], lw["d_norm"])


def _merge_kernel(u_ref, ya_ref, yb_ref, yc_ref, yd_ref, x_ref, wg_ref, wba_ref, wbb_ref, wbc_ref, wbd_ref,
                  wo_ref, gain_ref, h_ref, acc_ref):
    j = pl.program_id(1)

    @pl.when(j == 0)
    def _():
        acc_ref[...] = jnp.zeros_like(acc_ref)

    u = u_ref[...]
    merged = None
    for b, (y_r, wb_r) in enumerate(((ya_ref, wba_ref), (yb_ref, wbb_ref), (yc_ref, wbc_ref), (yd_ref, wbd_ref))):
        gate = _sigmoid(jnp.dot(u, wg_ref[b], preferred_element_type=F32))
        term = gate * jnp.dot(y_r[...], wb_r[...], preferred_element_type=F32)
        merged = term if merged is None else merged + term
    acc_ref[...] += jnp.dot(merged.astype(BF16), wo_ref[...], preferred_element_type=F32)

    @pl.when(j == pl.num_programs(1) - 1)
    def _():
        h_ref[...] = x_ref[...] + _rms(acc_ref[...], gain_ref[...])


def _merge(u, ya, yb, yc, yd, x2, lw):
    n = x2.shape[0]
    tm = min(n, 512)
    tn = 256
    row = lambda w: pl.BlockSpec((tm, w), lambda i, j: (i, 0))
    wcol = lambda k: pl.BlockSpec((k, tn), lambda i, j: (0, j))
    return pl.pallas_call(
        _merge_kernel,
        grid=(n // tm, D_MODEL // tn),
        in_specs=[row(D_MODEL), row(ya.shape[1]), row(yb.shape[1]), row(yc.shape[1]), row(yd.shape[1]),
                  row(D_MODEL),
                  pl.BlockSpec((4, D_MODEL, tn), lambda i, j: (0, 0, j)),
                  wcol(ya.shape[1]), wcol(yb.shape[1]), wcol(yc.shape[1]), wcol(yd.shape[1]),
                  pl.BlockSpec((tn, D_MODEL), lambda i, j: (j, 0)),
                  pl.BlockSpec((1, D_MODEL), lambda i, j: (0, 0))],
        out_specs=row(D_MODEL),
        out_shape=jax.ShapeDtypeStruct((n, D_MODEL), F32),
        scratch_shapes=[pltpu.VMEM((tm, D_MODEL), F32)],
        compiler_params=pltpu.CompilerParams(dimension_semantics=("parallel", "arbitrary"),
                                             vmem_limit_bytes=VMEM_LIMIT),
        name="merge",
    )(u, ya, yb, yc, yd, x2, lw["w_merge_gate"], lw["wb_a"], lw["wb_b"], lw["wb_c"], lw["wb_d"],
      lw["w_out"], lw["norm_mix_post"])


def _ffn_kernel(h_ref, g1_ref, wu_ref, wd_ref, g2_ref, o_ref, z_scr, acc_ref):
    j = pl.program_id(1)

    @pl.when(j == 0)
    def _():
        z_scr[...] = _rms(h_ref[...], g1_ref[...]).astype(BF16)
        acc_ref[...] = jnp.zeros_like(acc_ref)

    a = jnp.maximum(jnp.dot(z_scr[...], wu_ref[...], preferred_element_type=F32), 0.0)
    acc_ref[...] += jnp.dot((a * a).astype(BF16), wd_ref[...], preferred_element_type=F32)

    @pl.when(j == pl.num_programs(1) - 1)
    def _():
        o_ref[...] = h_ref[...] + _rms(acc_ref[...], g2_ref[...])


def _ffn(h, lw):
    n = h.shape[0]
    tm = min(n, 512)
    tf = 1024
    vec = pl.BlockSpec((1, D_MODEL), lambda i, j: (0, 0))
    return pl.pallas_call(
        _ffn_kernel,
        grid=(n // tm, D_FF // tf),
        in_specs=[pl.BlockSpec((tm, D_MODEL), lambda i, j: (i, 0)), vec,
                  pl.BlockSpec((D_MODEL, tf), lambda i, j: (0, j)),
                  pl.BlockSpec((tf, D_MODEL), lambda i, j: (j, 0)), vec],
        out_specs=pl.BlockSpec((tm, D_MODEL), lambda i, j: (i, 0)),
        out_shape=jax.ShapeDtypeStruct((n, D_MODEL), F32),
        scratch_shapes=[pltpu.VMEM((tm, D_MODEL), BF16), pltpu.VMEM((tm, D_MODEL), F32)],
        compiler_params=pltpu.CompilerParams(dimension_semantics=("parallel", "arbitrary"),
                                             vmem_limit_bytes=VMEM_LIMIT),
        name="ffn",
    )(h, lw["norm_ffn_pre"], lw["w_ffn_up"], lw["w_ffn_down"], lw["norm_ffn_post"])


def _band_bias_row(rel):
    m = np.arange(C_WIN)
    idx = np.where(m <= C_BAND + CHUNK, np.clip(C_BAND - m, -C_REL_CLIP, C_REL_CLIP) + C_REL_CLIP, 2 * C_REL_CLIP)
    return rel[:, idx.astype(np.int32)]


def _prep_layer(l, w):
    row = lambda v: v.reshape(1, -1)
    a_slot = lambda v: _take_cols(v, _A_SLOT_IDX)
    zeros64 = jnp.zeros((64, A_HEADS * LANES), F32)
    d_lane = np.full(LANES, -1, np.int32)
    d_lane[D_HEADS:2 * D_HEADS] = np.arange(D_HEADS)
    wb = w["w_branch"][l]
    return {
        "norm_mix_pre": row(w["norm_mix_pre"][l]), "norm_mix_post": row(w["norm_mix_post"][l]),
        "norm_ffn_pre": row(w["norm_ffn_pre"][l]), "norm_ffn_post": row(w["norm_ffn_post"][l]),
        "w_in": _take_cols(w["w_in"][l], _P_IDX).astype(BF16),
        "w_merge_gate": w["w_merge_gate"][l].astype(BF16),
        "wb_a": _take_rows(wb[0], _A_SLOT_IDX).astype(BF16), "wb_b": wb[1].astype(BF16),
        "wb_c": _take_rows(wb[2], _A_SLOT_IDX).astype(BF16), "wb_d": wb[3].astype(BF16),
        "w_out": w["w_out"][l].astype(BF16),
        "w_ffn_up": w["w_ffn_up"][l].astype(BF16), "w_ffn_down": w["w_ffn_down"][l].astype(BF16),
        "a_mu": _take_cols(row(w["a_mu"][l]), _A_ROW_IDX),
        "a_w0": a_slot(row(w["a_w0"][l])), "a_a0": a_slot(row(w["a_a0"][l])),
        "a_w_up": jnp.concatenate([a_slot(w["a_w_up"][l]), zeros64], axis=0).astype(BF16),
        "a_a_up": jnp.concatenate([zeros64, a_slot(w["a_a_up"][l])], axis=0).astype(BF16),
        "a_g_up": a_slot(w["a_g_up"][l]).astype(BF16),
        "a_k_k": a_slot(row(w["a_k_k"][l])), "a_k_a": a_slot(row(w["a_k_a"][l])),
        "a_r_k": a_slot(row(w["a_r_k"][l])),
        "a_ln_w": a_slot(row(w["a_ln_w"][l])), "a_ln_b": a_slot(row(w["a_ln_b"][l])),
        "b_alpha_up": jnp.pad(_take_cols(w["b_alpha_up"][l], _B_SLOT_IDX), ((0, LANES - 16), (0, 0))).astype(BF16),
        "b_alpha_bias": _take_cols(row(w["b_alpha_bias"][l]), _B_SLOT_IDX),
        "b_norm": row(w["b_norm"][l]),
        "c_gtab": _band_bias_row(w["c_rel_bias"][l]),
        "d_conv_w": w["d_conv_w"][l],
        "d_a_log": _take_cols(row(w["d_a_log"][l]), d_lane), "d_dt_bias": _take_cols(row(w["d_dt_bias"][l]), d_lane),
        "d_norm": row(w["d_norm"][l]),
    }


def _layer(x, st, lw):
    bsz, t, _ = x.shape
    n = bsz * t
    shift_prev, s_a, s_b, band_k, band_v, conv_buf, s_d = st
    x2 = x.reshape(n, D_MODEL)
    u, p = _inproj(x2, lw["norm_mix_pre"], lw["w_in"])
    p3 = p.reshape(bsz, t, P_COLS)

    def seg(name, width):
        return p3[:, :, _P_OFF[name]:_P_OFF[name] + width]

    y_a, s_a_new = _rwkv(p3, _take_cols(shift_prev, _A_ROW_IDX),
                         jnp.pad(s_a, ((0, 0), (0, 0), (0, LANES - A_HD), (0, LANES - A_HD))), lw)
    s_b_t = jnp.pad(jnp.swapaxes(s_b, -1, -2), ((0, 0), (0, 0), (0, 0), (0, LANES - B_DK)))
    y_b, s_b_new = _gla(p3, s_b_t, lw)
    hw = C_HEADS * LANES
    k_new, v_new = seg("c_k", hw), seg("c_v", hw)
    if band_k is None:
        pad_rows = C_BAND
        front = jnp.zeros((bsz, C_BAND, hw), BF16)
    else:
        pad_rows = 0
        slot = lambda z: jnp.pad(z, ((0, 0), (0, 0), (0, 0), (0, LANES - C_HD))).reshape(bsz, C_BAND, hw).astype(BF16)
        front = None
    tail = jnp.zeros((bsz, CHUNK, hw), BF16)
    kpad = jnp.concatenate([front if band_k is None else slot(band_k), k_new.astype(BF16), tail], axis=1)
    vpad = jnp.concatenate([front if band_v is None else slot(band_v), v_new.astype(BF16), tail], axis=1)
    y_c = _band(p3, kpad, vpad, lw["c_gtab"], pad_rows)
    y_d, s_d_new = _dn(p3, jnp.pad(conv_buf, ((0, 0), (8 - (D_CONV - 1), 0), (0, 0))), s_d, lw)

    h = _merge(u, y_a.reshape(n, -1), y_b.reshape(n, -1), y_c.reshape(n, -1), y_d.reshape(n, -1), x2, lw)
    out = _ffn(h, lw).reshape(bsz, t, D_MODEL)

    keep = min(C_BAND, t)
    unslot = lambda z: z.reshape(bsz, t, C_HEADS, LANES)[:, t - keep:, :, :C_HD]
    new_state = (
        jnp.take(p3[:, t - 1:, :], jnp.asarray(_A_ORIG_POS), axis=-1),
        s_a_new[:, :, :A_HD, :A_HD],
        jnp.swapaxes(s_b_new[:, :, :, :B_DK], -1, -2),
        unslot(k_new), unslot(v_new),
        seg("d_qkv", 3 * MIX_W)[:, t - (D_CONV - 1):, :],
        s_d_new,
    )
    return out, new_state


def kernel(x_prompt, x_sample, state_rwkv_shift, state_rwkv, state_gla, cache_band_k, cache_band_v, state_dn_conv, state_dn, norm_mix_pre, norm_mix_post, norm_ffn_pre, norm_ffn_post, w_in, w_merge_gate, w_branch, w_out, w_ffn_up, w_ffn_down, a_mu, a_w0, a_w_up, a_a0, a_a_up, a_g_up, a_k_k, a_k_a, a_r_k, a_ln_w, a_ln_b, b_alpha_up, b_alpha_bias, b_norm, c_rel_bias, d_conv_w, d_a_log, d_dt_bias, d_norm):
    weights = dict(norm_mix_pre=norm_mix_pre, norm_mix_post=norm_mix_post, norm_ffn_pre=norm_ffn_pre,
                   norm_ffn_post=norm_ffn_post, w_in=w_in, w_merge_gate=w_merge_gate, w_branch=w_branch,
                   w_out=w_out, w_ffn_up=w_ffn_up, w_ffn_down=w_ffn_down, a_mu=a_mu, a_w0=a_w0, a_w_up=a_w_up,
                   a_a0=a_a0, a_a_up=a_a_up, a_g_up=a_g_up, a_k_k=a_k_k, a_k_a=a_k_a,
                   a_r_k=a_r_k.reshape(a_r_k.shape[0], -1), a_ln_w=a_ln_w, a_ln_b=a_ln_b, b_alpha_up=b_alpha_up,
                   b_alpha_bias=b_alpha_bias, b_norm=b_norm, c_rel_bias=c_rel_bias, d_conv_w=d_conv_w,
                   d_a_log=d_a_log, d_dt_bias=d_dt_bias, d_norm=d_norm)
    bsz = x_prompt.shape[0]
    depth = w_in.shape[0]
    y_p, y_s = x_prompt, x_sample
    new_p, new_s = [], []
    for l in range(depth):
        lw = _prep_layer(l, weights)
        zero_state = (jnp.zeros((bsz, 1, A_IN), F32), jnp.zeros((bsz, A_HEADS, A_HD, A_HD), F32),
                      jnp.zeros((bsz, B_HEADS, B_DK, B_DV), F32), None, None,
                      jnp.zeros((bsz, D_CONV - 1, 3 * MIX_W), F32), jnp.zeros((bsz, D_HEADS, D_HD, D_HD), F32))
        y_p, st_p = _layer(y_p, zero_state, lw)
        y_s, st_s = _layer(y_s, (state_rwkv_shift[l], state_rwkv[l], state_gla[l], cache_band_k[l],
                                 cache_band_v[l], state_dn_conv[l], state_dn[l]), lw)
        new_p.append(st_p)
        new_s.append(st_s)
    stack = lambda states, i: jnp.stack([s[i] for s in states])
    return ((y_p, y_s) + tuple(stack(new_p, i) for i in range(7)) + tuple(stack(new_s, i) for i in range(7)))
```

```python
import functools

import numpy as np
import jax
import jax.numpy as jnp
from jax import lax
from jax.experimental import pallas as pl
from jax.experimental.pallas import tpu as pltpu

F32 = jnp.float32
BF16 = jnp.bfloat16

D_MODEL = 2048
MIX_W = 512
D_FF = 4 * D_MODEL
CHUNK = 64
LANES = 128
NORM_EPS = 1e-6
VMEM_LIMIT = 56 * 1024 * 1024

A_HEADS, A_HD = 8, 64
A_DECAY_SCALE = 0.6065306597
A_GN_EPS = 64e-5
A_IN = 1792
B_HEADS, B_DK, B_DV = 4, 64, 128
B_GATE_TEMP = 16.0
B_IN = 1552
C_HEADS, C_HD = 8, 64
C_BAND = 8 * CHUNK
C_REL_CLIP = 2 * CHUNK
C_WIN = C_BAND + 2 * CHUNK
C_IN = 1536
D_HEADS, D_HD = 4, 128
D_CONV = 4
D_IN = 2056
IN_TOTAL = A_IN + B_IN + C_IN + D_IN


def _head_pad(cols, head):
    out = []
    for h in range(len(cols) // head):
        out.extend(cols[h * head:(h + 1) * head])
        out.extend([-1] * (LANES - head))
    return out


def _pad_to(cols, width):
    return list(cols) + [-1] * (width - len(cols))


def _build_layout():
    a0, b0, c0, d0 = 0, A_IN, A_IN + B_IN, A_IN + B_IN + C_IN
    rng = lambda s, n: list(range(s, s + n))
    segs = [
        ("a_r", _head_pad(rng(a0, 512), 64)), ("a_k", _head_pad(rng(a0 + 512, 512), 64)),
        ("a_v", _head_pad(rng(a0 + 1024, 512), 64)),
        ("c_q", _head_pad(rng(c0, 512), 64)), ("c_k", _head_pad(rng(c0 + 512, 512), 64)),
        ("c_v", _head_pad(rng(c0 + 1024, 512), 64)),
        ("d_qkv", rng(d0, 1536)), ("d_gate", rng(d0 + 1544, 512)),
        ("b_q", _head_pad(rng(b0, 256), 64)), ("b_k", _head_pad(rng(b0 + 256, 256), 64)),
        ("b_v", rng(b0 + 512, 512)), ("b_rg", rng(b0 + 1040, 512)),
        ("a_wa", rng(a0 + 1536, 128)), ("a_gl", rng(a0 + 1664, 128)),
        ("b_al", _pad_to(rng(b0 + 1024, 16), LANES)), ("d_ba", _pad_to(rng(d0 + 1536, 8), LANES)),
    ]
    idx, off = [], {}
    for name, cols in segs:
        off[name] = len(idx)
        idx.extend(cols)
    return np.asarray(idx, np.int32), off


_P_IDX, _P_OFF = _build_layout()
P_COLS = int(_P_IDX.shape[0])
_A_SLOT_IDX = np.asarray(_head_pad(list(range(512)), 64), np.int32)
_B_SLOT_IDX = np.asarray(_head_pad(list(range(256)), 64), np.int32)
_A_ROW_IDX = np.concatenate([_A_SLOT_IDX, np.where(_A_SLOT_IDX >= 0, _A_SLOT_IDX + 512, -1),
                             np.where(_A_SLOT_IDX >= 0, _A_SLOT_IDX + 1024, -1),
                             np.arange(1536, 1792, dtype=np.int32)]).astype(np.int32)
A_ROW = int(_A_ROW_IDX.shape[0])


def _take_static(x, idx, axis):
    idx = np.asarray(idx)
    axis = axis % x.ndim
    pieces, i = [], 0
    while i < len(idx):
        j = i + 1
        if idx[i] < 0:
            while j < len(idx) and idx[j] < 0:
                j += 1
            pieces.append(jnp.zeros(x.shape[:axis] + (j - i,) + x.shape[axis + 1:], x.dtype))
        else:
            while j < len(idx) and idx[j] == idx[j - 1] + 1:
                j += 1
            pieces.append(lax.slice_in_dim(x, int(idx[i]), int(idx[i]) + (j - i), axis=axis))
        i = j
    return pieces[0] if len(pieces) == 1 else jnp.concatenate(pieces, axis=axis)


def _take_cols(x, idx):
    return _take_static(x, idx, -1)


def _take_rows(x, idx):
    return _take_static(x, idx, -2)


def _a_row_to_orig():
    pos = np.zeros(A_IN, np.int32)
    base = {0: _P_OFF["a_r"], 1: _P_OFF["a_k"], 2: _P_OFF["a_v"]}
    for i in range(1536):
        part, j = divmod(i, 512)
        pos[i] = base[part] + (j // 64) * LANES + (j % 64)
    pos[1536:1664] = _P_OFF["a_wa"] + np.arange(128)
    pos[1664:1792] = _P_OFF["a_gl"] + np.arange(128)
    return pos


_A_ORIG_POS = _a_row_to_orig()


def _mm(a, b):
    return jnp.dot(a.astype(BF16), b.astype(BF16), preferred_element_type=F32)


def _mm_nt(a, b):
    return lax.dot_general(a.astype(BF16), b.astype(BF16), (((1,), (1,)), ((), ())),
                           preferred_element_type=F32)


def _split(x):
    hi = x.astype(BF16)
    lo = (x - hi.astype(F32)).astype(BF16)
    return hi, lo


def _mm3(a, b):
    ah, al = _split(a)
    bh, bl = _split(b)
    return (jnp.dot(ah, bh, preferred_element_type=F32) + jnp.dot(ah, bl, preferred_element_type=F32)
            + jnp.dot(al, bh, preferred_element_type=F32))


def _mm_exact_lhs(a_bf16, b):
    bh, bl = _split(b)
    return jnp.dot(a_bf16, bh, preferred_element_type=F32) + jnp.dot(a_bf16, bl, preferred_element_type=F32)


def _iota2(shape, axis):
    return lax.broadcasted_iota(jnp.int32, shape, axis)


def _tri_inverse(a):
    n = a.shape[0]
    eye = (_iota2((n, n), 0) == _iota2((n, n), 1)).astype(F32)
    t = eye + a
    x = a
    for _ in range(int(np.log2(n)) - 1):
        x = _mm3(x, x)
        t = t + _mm3(t, x)
    return t


def _tri_inverse_many(mats):
    n = mats[0].shape[0]
    eye = (_iota2((n, n), 0) == _iota2((n, n), 1)).astype(F32)
    ts = [eye + a for a in mats]
    xs = list(mats)
    for _ in range(int(np.log2(n)) - 1):
        xs = [_mm3(x, x) for x in xs]
        ts = [t + _mm3(t, x) for t, x in zip(ts, xs)]
    return ts


def _sigmoid(x):
    return 1.0 / (1.0 + jnp.exp(-x))


def _softplus(x):
    return jnp.maximum(x, 0.0) + jnp.log(1.0 + jnp.exp(-jnp.abs(x)))


def _rms(x, gain):
    return x * lax.rsqrt(jnp.mean(x * x, axis=-1, keepdims=True) + NORM_EPS) * gain


def _lower_tri_bf16(n):
    return (_iota2((n, n), 0) >= _iota2((n, n), 1)).astype(BF16)


def _shift_rows(x, s, carry_rows):
    rows = _iota2(x.shape, 0)
    out = pltpu.roll(x, s, 0)
    for t in range(s):
        out = jnp.where(rows == t, jnp.broadcast_to(carry_rows[t], x.shape), out)
    return out


def _inproj_kernel(x_ref, g_ref, w_ref, u_ref, p_ref, u_scr):
    @pl.when(pl.program_id(1) == 0)
    def _():
        u = _rms(x_ref[...], g_ref[...]).astype(BF16)
        u_scr[...] = u
        u_ref[...] = u

    p_ref[...] = jnp.dot(u_scr[...], w_ref[...], preferred_element_type=F32)


def _inproj(x2, gain, w):
    n = x2.shape[0]
    tm = min(n, 1024)
    tn = 512
    return pl.pallas_call(
        _inproj_kernel,
        grid=(n // tm, P_COLS // tn),
        in_specs=[pl.BlockSpec((tm, D_MODEL), lambda i, j: (i, 0)),
                  pl.BlockSpec((1, D_MODEL), lambda i, j: (0, 0)),
                  pl.BlockSpec((D_MODEL, tn), lambda i, j: (0, j))],
        out_specs=[pl.BlockSpec((tm, D_MODEL), lambda i, j: (i, 0)),
                   pl.BlockSpec((tm, tn), lambda i, j: (i, j))],
        out_shape=[jax.ShapeDtypeStruct((n, D_MODEL), BF16), jax.ShapeDtypeStruct((n, P_COLS), F32)],
        scratch_shapes=[pltpu.VMEM((tm, D_MODEL), BF16)],
        compiler_params=pltpu.CompilerParams(dimension_semantics=("parallel", "arbitrary"),
                                             vmem_limit_bytes=VMEM_LIMIT),
        name="inproj",
    )(x2, gain, w)


def _rwkv_kernel(pr_ref, pk_ref, pv_ref, pwa_ref, pgl_ref, sh0_ref, s0_ref, mu_ref, w0_ref, wup_ref,
                 a0_ref, aup_ref, gup_ref, kk_ref, ka_ref, rk_ref, lnw_ref, lnb_ref,
                 y_ref, sout_ref, carry_scr, s_scr):
    c = pl.program_id(1)
    C = CHUNK
    HW = A_HEADS * LANES

    @pl.when(c == 0)
    def _():
        carry_scr[...] = sh0_ref[...]
        s_scr[...] = s0_ref[...]

    def token_shift(p_ref, lo, hi):
        p = p_ref[...]
        prev = _shift_rows(p, 1, [carry_scr[:, lo:hi]])
        carry_scr[:, lo:hi] = p[C - 1:C, :]
        return p + (prev - p) * mu_ref[:, lo:hi]

    xr = token_shift(pr_ref, 0, HW)
    xk = token_shift(pk_ref, HW, 2 * HW)
    xv = token_shift(pv_ref, 2 * HW, 3 * HW)
    xwa = token_shift(pwa_ref, 3 * HW, 3 * HW + LANES)
    xgl = token_shift(pgl_ref, 3 * HW + LANES, 3 * HW + 2 * LANES)

    logw = -A_DECAY_SCALE * _sigmoid(w0_ref[...] + _mm(jnp.tanh(xwa), wup_ref[...]))
    a = _sigmoid(a0_ref[...] + _mm(xwa, aup_ref[...]))
    g = _mm(_sigmoid(xgl), gup_ref[...])
    kkv = xk * kk_ref[...]
    kmod = xk * (1.0 + (a - 1.0) * ka_ref[...])
    cum = _mm_exact_lhs(_lower_tri_bf16(C), logw)

    ri, ci = _iota2((2 * C, C), 0), _iota2((2 * C, C), 1)
    mask2 = jnp.logical_or(ri % C > ci, jnp.logical_and(ri >= C, ri - C == ci))
    real = _iota2((C, LANES), 1) < A_HD
    heads = range(A_HEADS)
    sls = [slice(h * LANES, (h + 1) * LANES) for h in heads]

    lhs2, bhs, khs, bks, vs = [], [], [], [], []
    for sl in sls:
        kk_h = kkv[:, sl]
        kk_h = kk_h * lax.rsqrt(jnp.sum(kk_h * kk_h, axis=-1, keepdims=True) + 1e-6)
        cum_h = cum[:, sl]
        g_incl = jnp.exp(cum_h)
        g_inv = jnp.exp(-cum_h)
        g_last = g_incl[C - 1:C, :]
        at = kk_h * jnp.exp(cum_h - logw[:, sl])
        rt = xr[:, sl] * g_incl
        bh = -(kk_h * a[:, sl]) * g_inv
        kh = kmod[:, sl] * g_inv
        lhs2.append(jnp.concatenate([at, rt], axis=0).astype(BF16))
        bhs.append(bh.astype(BF16))
        khs.append(kh.astype(BF16))
        bks.append(jnp.concatenate([bh * g_last, kh * g_last], axis=0).astype(BF16))
        vs.append(xv[:, sl])
    s_old = [s_scr[h] for h in heads]
    aab = [jnp.where(mask2, _mm_nt(lhs2[h], bhs[h]), 0.0) for h in heads]
    aak = [jnp.where(mask2, _mm_nt(lhs2[h], khs[h]), 0.0) for h in heads]
    p0 = [_mm_nt(lhs2[h], s_old[h]) for h in heads]
    t_inv = _tri_inverse_many([aab[h][0:C] for h in heads])
    rhs_u = [p0[h][0:C] + _mm(aak[h][0:C], vs[h]) for h in heads]
    us = [_mm(t_inv[h], rhs_u[h]) for h in heads]
    ys = [p0[h][C:2 * C] + _mm(aab[h][C:2 * C], us[h]) + _mm(aak[h][C:2 * C], vs[h]) for h in heads]
    for h in heads:
        z = jnp.concatenate([us[h], vs[h]], axis=0)
        g_last = jnp.exp(cum[C - 1:C, sls[h]])
        s_scr[h] = s_old[h] * g_last + _mm(z.T, bks[h])
    for h, sl in zip(heads, sls):
        y = ys[h]
        mean = jnp.sum(y, axis=-1, keepdims=True) * (1.0 / A_HD)
        d = jnp.where(real, y - mean, 0.0)
        var = jnp.sum(d * d, axis=-1, keepdims=True) * (1.0 / A_HD)
        yn = d * lax.rsqrt(var + A_GN_EPS) * lnw_ref[:, sl] + lnb_ref[:, sl]
        bonus = jnp.sum(xr[:, sl] * kmod[:, sl] * rk_ref[:, sl], axis=-1, keepdims=True) * vs[h]
        y_ref[:, sl] = ((yn + bonus) * g[:, sl]).astype(BF16)

    @pl.when(c == pl.num_programs(1) - 1)
    def _():
        sout_ref[...] = s_scr[...]


def _rwkv(p3, shift_prev, s0, lw):
    bsz, t, _ = p3.shape
    hw = A_HEADS * LANES
    blk = lambda w, j: pl.BlockSpec((None, CHUNK, w), lambda b, c: (b, c, j))
    full = lambda shp: pl.BlockSpec(shp, lambda b, c: (0,) * len(shp))
    return pl.pallas_call(
        _rwkv_kernel,
        grid=(bsz, t // CHUNK),
        in_specs=[blk(hw, _P_OFF["a_r"] // hw), blk(hw, _P_OFF["a_k"] // hw), blk(hw, _P_OFF["a_v"] // hw),
                  blk(LANES, _P_OFF["a_wa"] // LANES), blk(LANES, _P_OFF["a_gl"] // LANES),
                  pl.BlockSpec((None, 1, A_ROW), lambda b, c: (b, 0, 0)),
                  pl.BlockSpec((None, A_HEADS, LANES, LANES), lambda b, c: (b, 0, 0, 0)),
                  full((1, A_ROW)), full((1, hw)), full((LANES, hw)), full((1, hw)), full((LANES, hw)),
                  full((LANES, hw)), full((1, hw)), full((1, hw)), full((1, hw)), full((1, hw)), full((1, hw))],
        out_specs=[pl.BlockSpec((None, CHUNK, hw), lambda b, c: (b, c, 0)),
                   pl.BlockSpec((None, A_HEADS, LANES, LANES), lambda b, c: (b, 0, 0, 0))],
        out_shape=[jax.ShapeDtypeStruct((bsz, t, hw), BF16),
                   jax.ShapeDtypeStruct((bsz, A_HEADS, LANES, LANES), F32)],
        scratch_shapes=[pltpu.VMEM((1, A_ROW), F32), pltpu.VMEM((A_HEADS, LANES, LANES), F32)],
        compiler_params=pltpu.CompilerParams(dimension_semantics=("parallel", "arbitrary"),
                                             vmem_limit_bytes=VMEM_LIMIT),
        name="rwkv7",
    )(p3, p3, p3, p3, p3, shift_prev, s0, lw["a_mu"], lw["a_w0"], lw["a_w_up"], lw["a_a0"], lw["a_a_up"],
      lw["a_g_up"], lw["a_k_k"], lw["a_k_a"], lw["a_r_k"], lw["a_ln_w"], lw["a_ln_b"])


def _gla_kernel(q_ref, k_ref, v_ref, rg_ref, al_ref, s0_ref, aup_ref, ab_ref, nw_ref,
                y_ref, sout_ref, s_scr):
    c = pl.program_id(1)
    C = CHUNK

    @pl.when(c == 0)
    def _():
        s_scr[...] = s0_ref[...]

    x = _mm(al_ref[...], aup_ref[...]) + ab_ref[...]
    loga = -_softplus(-x) * (1.0 / B_GATE_TEMP)
    cum = _mm_exact_lhs(_lower_tri_bf16(C), loga)
    causal = _iota2((C, C), 0) >= _iota2((C, C), 1)
    q, k, v, rg = q_ref[...], k_ref[...], v_ref[...], rg_ref[...]

    heads = range(B_HEADS)
    sls = [slice(h * LANES, (h + 1) * LANES) for h in heads]
    qts, kts, kds, vhs, elast = [], [], [], [], []
    for sl in sls:
        b = cum[:, sl]
        b_last = b[C - 1:C, :]
        qts.append((q[:, sl] * (B_DK ** -0.5) * jnp.exp(b)).astype(BF16))
        kts.append((k[:, sl] * jnp.exp(-b)).astype(BF16))
        kds.append((k[:, sl] * jnp.exp(b_last - b)).astype(BF16))
        vhs.append(v[:, sl].astype(BF16))
        elast.append(jnp.exp(b_last))
    s_old = [s_scr[h] for h in heads]
    atts = [jnp.where(causal, _mm_nt(qts[h], kts[h]), 0.0) for h in heads]
    inter = [_mm_nt(qts[h], s_old[h]) for h in heads]
    outs = [inter[h] + _mm(atts[h], vhs[h]) for h in heads]
    zeros = jnp.zeros((C, LANES), BF16)
    for h in heads:
        vt = jnp.concatenate([v[:, sls[h]], jnp.zeros((C, LANES), F32)], axis=0).T
        s_scr[h] = s_old[h] * elast[h] + _mm(vt, jnp.concatenate([kds[h], zeros], axis=0))
    for h, sl in zip(heads, sls):
        o = outs[h]
        o = o * lax.rsqrt(jnp.mean(o * o, axis=-1, keepdims=True) + NORM_EPS)
        rg_h = rg[:, sl]
        y_ref[:, sl] = (o * nw_ref[:, sl] * (rg_h * _sigmoid(rg_h))).astype(BF16)

    @pl.when(c == pl.num_programs(1) - 1)
    def _():
        sout_ref[...] = s_scr[...]


def _gla(p3, s0, lw):
    bsz, t, _ = p3.shape
    blk = lambda w, j: pl.BlockSpec((None, CHUNK, w), lambda b, c: (b, c, j))
    full = lambda shp: pl.BlockSpec(shp, lambda b, c: (0,) * len(shp))
    st = pl.BlockSpec((None, B_HEADS, LANES, LANES), lambda b, c: (b, 0, 0, 0))
    return pl.pallas_call(
        _gla_kernel,
        grid=(bsz, t // CHUNK),
        in_specs=[blk(MIX_W, _P_OFF["b_q"] // MIX_W), blk(MIX_W, _P_OFF["b_k"] // MIX_W),
                  blk(MIX_W, _P_OFF["b_v"] // MIX_W), blk(MIX_W, _P_OFF["b_rg"] // MIX_W),
                  blk(LANES, _P_OFF["b_al"] // LANES), st,
                  full((LANES, MIX_W)), full((1, MIX_W)), full((1, MIX_W))],
        out_specs=[pl.BlockSpec((None, CHUNK, MIX_W), lambda b, c: (b, c, 0)), st],
        out_shape=[jax.ShapeDtypeStruct((bsz, t, MIX_W), BF16),
                   jax.ShapeDtypeStruct((bsz, B_HEADS, LANES, LANES), F32)],
        scratch_shapes=[pltpu.VMEM((B_HEADS, LANES, LANES), F32)],
        compiler_params=pltpu.CompilerParams(dimension_semantics=("parallel", "arbitrary"),
                                             vmem_limit_bytes=VMEM_LIMIT),
        name="gla",
    )(p3, p3, p3, p3, p3, s0, lw["b_alpha_up"], lw["b_alpha_bias"], lw["b_norm"])


def _band_kernel(q_ref, k_ref, v_ref, g_ref, y_ref, bias_scr, *, pad_rows):
    b = pl.program_id(0)
    c = pl.program_id(1)
    C = CHUNK

    @pl.when(jnp.logical_and(b == 0, c == 0))
    def _():
        rows = _iota2((C, C_WIN), 0)
        for h in range(C_HEADS):
            x = jnp.broadcast_to(g_ref[h:h + 1, :], (C, C_WIN))
            for bit in range(6):
                x = jnp.where(((rows >> bit) & 1) == 1, pltpu.roll(x, 1 << bit, 1), x)
            bias_scr[h] = x

    start = pl.multiple_of(c * C, C)
    col = _iota2((C, C_WIN), 1)
    valid = jnp.logical_and(col < C_BAND + C, col + c * C >= pad_rows)
    heads = range(C_HEADS)
    sls = [slice(h * LANES, (h + 1) * LANES) for h in heads]
    scores = [_mm_nt((q_ref[:, sl] * (C_HD ** -0.5)).astype(BF16), k_ref[pl.ds(start, C_WIN), sl]) for sl in sls]
    probs, denoms = [], []
    for h in heads:
        s = jnp.where(valid, scores[h] + bias_scr[h], -jnp.inf)
        p = jnp.exp(s - jnp.max(s, axis=-1, keepdims=True))
        denoms.append(jnp.sum(p, axis=-1, keepdims=True))
        probs.append(p.astype(BF16))
    outs = [_mm(probs[h], v_ref[pl.ds(start, C_WIN), sls[h]]) for h in heads]
    for h, sl in zip(heads, sls):
        y_ref[:, sl] = (outs[h] / denoms[h]).astype(BF16)


def _band(p3, kpad, vpad, gtab, pad_rows):
    bsz, t, _ = p3.shape
    hw = C_HEADS * LANES
    tp = kpad.shape[1]
    kv = pl.BlockSpec((None, tp, hw), lambda b, c: (b, 0, 0))
    return pl.pallas_call(
        functools.partial(_band_kernel, pad_rows=pad_rows),
        grid=(bsz, t // CHUNK),
        in_specs=[pl.BlockSpec((None, CHUNK, hw), lambda b, c: (b, c, _P_OFF["c_q"] // hw)), kv, kv,
                  pl.BlockSpec((C_HEADS, C_WIN), lambda b, c: (0, 0))],
        out_specs=pl.BlockSpec((None, CHUNK, hw), lambda b, c: (b, c, 0)),
        out_shape=jax.ShapeDtypeStruct((bsz, t, hw), BF16),
        scratch_shapes=[pltpu.VMEM((C_HEADS, CHUNK, C_WIN), F32)],
        compiler_params=pltpu.CompilerParams(dimension_semantics=("arbitrary", "arbitrary"),
                                             vmem_limit_bytes=VMEM_LIMIT),
        name="band_attn",
    )(p3, kpad, vpad, gtab)


def _dn_kernel(qkv_ref, ba_ref, gate_ref, cb0_ref, s0_ref, cw_ref, alog_ref, dtb_ref, nw_ref,
               y_ref, sout_ref, carry_scr, s_scr):
    c = pl.program_id(1)
    C = CHUNK

    @pl.when(c == 0)
    def _():
        carry_scr[...] = cb0_ref[...]
        s_scr[...] = s0_ref[...]

    x = qkv_ref[...]
    c0, c1, c2 = carry_scr[5:6, :], carry_scr[6:7, :], carry_scr[7:8, :]
    conv = (_shift_rows(x, 3, [c0, c1, c2]) * cw_ref[0:1, :] + _shift_rows(x, 2, [c1, c2]) * cw_ref[1:2, :]
            + _shift_rows(x, 1, [c2]) * cw_ref[2:3, :] + x * cw_ref[3:4, :])
    carry_scr[...] = x[C - 8:C, :]
    qkv = conv * _sigmoid(conv)

    ba = ba_ref[...]
    beta_all = _sigmoid(ba)
    g_all = -jnp.exp(alog_ref[...]) * _softplus(ba + dtb_ref[...])
    tri = _lower_tri_bf16(C)
    cum_all = _mm_exact_lhs(tri, g_all)
    lane = _iota2((C, LANES), 1)
    col_of = lambda arr, j: jnp.sum(jnp.where(lane == j, arr, 0.0), axis=-1, keepdims=True)
    ri, ci = _iota2((C, C), 0), _iota2((C, C), 1)
    after = (ri > ci).astype(F32)
    gate = gate_ref[...]

    heads = range(D_HEADS)
    sls = [slice(h * LANES, (h + 1) * LANES) for h in heads]
    betas = [col_of(beta_all, h) for h in heads]
    cums = [col_of(cum_all, D_HEADS + h) for h in heads]
    diffs = [_mm_exact_lhs(tri, col_of(g_all, D_HEADS + h) * after) for h in heads]
    decays = [jnp.where(ri >= ci, jnp.exp(diffs[h]), 0.0) for h in heads]
    qs, ks, vs = [], [], []
    for h in heads:
        q_h = qkv[:, h * LANES:(h + 1) * LANES]
        k_h = qkv[:, MIX_W + h * LANES:MIX_W + (h + 1) * LANES]
        qs.append(q_h * lax.rsqrt(jnp.sum(q_h * q_h, axis=-1, keepdims=True) + 1e-6) * (D_HD ** -0.5))
        ks.append(k_h * lax.rsqrt(jnp.sum(k_h * k_h, axis=-1, keepdims=True) + 1e-6))
        vs.append(qkv[:, 2 * MIX_W + h * LANES:2 * MIX_W + (h + 1) * LANES])
    kbf = [k.astype(BF16) for k in ks]
    kks = [_mm_nt(kbf[h], kbf[h]) for h in heads]
    atts = [_mm_nt(qs[h], kbf[h]) * decays[h] for h in heads]
    t_inv = _tri_inverse_many([jnp.where(ri > ci, -(betas[h] * decays[h] * kks[h]), 0.0) for h in heads])
    ecums = [jnp.exp(cums[h]) for h in heads]
    uws = [_mm(t_inv[h], jnp.concatenate([vs[h] * betas[h], ks[h] * (betas[h] * ecums[h])], axis=1)) for h in heads]
    s_old = [s_scr[h] for h in heads]
    deltas = [uws[h][:, 0:LANES] - _mm(uws[h][:, LANES:2 * LANES], s_old[h]) for h in heads]
    outs = [_mm(qs[h] * ecums[h], s_old[h]) + _mm(atts[h], deltas[h]) for h in heads]
    zeros = jnp.zeros((C, LANES), F32)
    for h in heads:
        cum_last = cums[h][C - 1:C, :]
        kdt = jnp.concatenate([ks[h] * jnp.exp(cum_last - cums[h]), zeros], axis=0).T
        s_scr[h] = s_old[h] * jnp.exp(cum_last) + _mm(kdt, jnp.concatenate([deltas[h], zeros], axis=0))
    for h, sl in zip(heads, sls):
        o = outs[h]
        o = o * lax.rsqrt(jnp.mean(o * o, axis=-1, keepdims=True) + NORM_EPS) * nw_ref[...]
        gt = gate[:, sl]
        y_ref[:, sl] = (o * (gt * _sigmoid(gt))).astype(BF16)

    @pl.when(c == pl.num_programs(1) - 1)
    def _():
        sout_ref[...] = s_scr[...]


def _dn(p3, conv_buf, s0, lw):
    bsz, t, _ = p3.shape
    w3 = 3 * MIX_W
    full = lambda shp: pl.BlockSpec(shp, lambda b, c: (0,) * len(shp))
    st = pl.BlockSpec((None, D_HEADS, LANES, LANES), lambda b, c: (b, 0, 0, 0))
    return pl.pallas_call(
        _dn_kernel,
        grid=(bsz, t // CHUNK),
        in_specs=[pl.BlockSpec((None, CHUNK, w3), lambda b, c: (b, c, _P_OFF["d_qkv"] // w3)),
                  pl.BlockSpec((None, CHUNK, LANES), lambda b, c: (b, c, _P_OFF["d_ba"] // LANES)),
                  pl.BlockSpec((None, CHUNK, MIX_W), lambda b, c: (b, c, _P_OFF["d_gate"] // MIX_W)),
                  pl.BlockSpec((None, 8, w3), lambda b, c: (b, 0, 0)), st,
                  full((D_CONV, w3)), full((1, LANES)), full((1, LANES)), full((1, LANES))],
        out_specs=[pl.BlockSpec((None, CHUNK, MIX_W), lambda b, c: (b, c, 0)), st],
        out_shape=[jax.ShapeDtypeStruct((bsz, t, MIX_W), BF16),
                   jax.ShapeDtypeStruct((bsz, D_HEADS, LANES, LANES), F32)],
        scratch_shapes=[pltpu.VMEM((8, w3), F32), pltpu.VMEM((D_HEADS, LANES, LANES), F32)],
        compiler_params=pltpu.CompilerParams(dimension_semantics=("parallel", "arbitrary"),
                                             vmem_limit_bytes=VMEM_LIMIT),
        name="deltanet",
    )(p3, p3, p3, conv_buf, s0, lw["d_conv_w"], lw["d_a_log"], lw["d_dt_bias"], lw["d_norm"])


def _merge_kernel(u_ref, ya_ref, yb_ref, yc_ref, yd_ref, x_ref, wg_ref, wba_ref, wbb_ref, wbc_ref, wbd_ref,
                  wo_ref, gain_ref, h_ref, acc_ref):
    j = pl.program_id(1)

    @pl.when(j == 0)
    def _():
        acc_ref[...] = jnp.zeros_like(acc_ref)

    u = u_ref[...]
    merged = None
    for b, (y_r, wb_r) in enumerate(((ya_ref, wba_ref), (yb_ref, wbb_ref), (yc_ref, wbc_ref), (yd_ref, wbd_ref))):
        gate = _sigmoid(jnp.dot(u, wg_ref[b], preferred_element_type=F32))
        term = gate * jnp.dot(y_r[...], wb_r[...], preferred_element_type=F32)
        merged = term if merged is None else merged + term
    acc_ref[...] += jnp.dot(merged.astype(BF16), wo_ref[...], preferred_element_type=F32)

    @pl.when(j == pl.num_programs(1) - 1)
    def _():
        h_ref[...] = x_ref[...] + _rms(acc_ref[...], gain_ref[...])


def _merge(u, ya, yb, yc, yd, x2, lw):
    n = x2.shape[0]
    tm = min(n, 512)
    tn = 256
    row = lambda w: pl.BlockSpec((tm, w), lambda i, j: (i, 0))
    wcol = lambda k: pl.BlockSpec((k, tn), lambda i, j: (0, j))
    return pl.pallas_call(
        _merge_kernel,
        grid=(n // tm, D_MODEL // tn),
        in_specs=[row(D_MODEL), row(ya.shape[1]), row(yb.shape[1]), row(yc.shape[1]), row(yd.shape[1]),
                  row(D_MODEL),
                  pl.BlockSpec((4, D_MODEL, tn), lambda i, j: (0, 0, j)),
                  wcol(ya.shape[1]), wcol(yb.shape[1]), wcol(yc.shape[1]), wcol(yd.shape[1]),
                  pl.BlockSpec((tn, D_MODEL), lambda i, j: (j, 0)),
                  pl.BlockSpec((1, D_MODEL), lambda i, j: (0, 0))],
        out_specs=row(D_MODEL),
        out_shape=jax.ShapeDtypeStruct((n, D_MODEL), F32),
        scratch_shapes=[pltpu.VMEM((tm, D_MODEL), F32)],
        compiler_params=pltpu.CompilerParams(dimension_semantics=("parallel", "arbitrary"),
                                             vmem_limit_bytes=VMEM_LIMIT),
        name="merge",
    )(u, ya, yb, yc, yd, x2, lw["w_merge_gate"], lw["wb_a"], lw["wb_b"], lw["wb_c"], lw["wb_d"],
      lw["w_out"], lw["norm_mix_post"])


def _ffn_kernel(h_ref, g1_ref, wu_ref, wd_ref, g2_ref, o_ref, z_scr, acc_ref):
    j = pl.program_id(1)

    @pl.when(j == 0)
    def _():
        z_scr[...] = _rms(h_ref[...], g1_ref[...]).astype(BF16)
        acc_ref[...] = jnp.zeros_like(acc_ref)

    a = jnp.maximum(jnp.dot(z_scr[...], wu_ref[...], preferred_element_type=F32), 0.0)
    acc_ref[...] += jnp.dot((a * a).astype(BF16), wd_ref[...], preferred_element_type=F32)

    @pl.when(j == pl.num_programs(1) - 1)
    def _():
        o_ref[...] = h_ref[...] + _rms(acc_ref[...], g2_ref[...])


def _ffn(h, lw):
    n = h.shape[0]
    tm = min(n, 512)
    tf = 1024
    vec = pl.BlockSpec((1, D_MODEL), lambda i, j: (0, 0))
    return pl.pallas_call(
        _ffn_kernel,
        grid=(n // tm, D_FF // tf),
        in_specs=[pl.BlockSpec((tm, D_MODEL), lambda i, j: (i, 0)), vec,
                  pl.BlockSpec((D_MODEL, tf), lambda i, j: (0, j)),
                  pl.BlockSpec((tf, D_MODEL), lambda i, j: (j, 0)), vec],
        out_specs=pl.BlockSpec((tm, D_MODEL), lambda i, j: (i, 0)),
        out_shape=jax.ShapeDtypeStruct((n, D_MODEL), F32),
        scratch_shapes=[pltpu.VMEM((tm, D_MODEL), BF16), pltpu.VMEM((tm, D_MODEL), F32)],
        compiler_params=pltpu.CompilerParams(dimension_semantics=("parallel", "arbitrary"),
                                             vmem_limit_bytes=VMEM_LIMIT),
        name="ffn",
    )(h, lw["norm_ffn_pre"], lw["w_ffn_up"], lw["w_ffn_down"], lw["norm_ffn_post"])


def _band_bias_row(rel):
    m = np.arange(C_WIN)
    idx = np.where(m <= C_BAND + CHUNK, np.clip(C_BAND - m, -C_REL_CLIP, C_REL_CLIP) + C_REL_CLIP, 2 * C_REL_CLIP)
    return rel[:, idx.astype(np.int32)]


def _prep_layer(l, w):
    row = lambda v: v.reshape(1, -1)
    a_slot = lambda v: _take_cols(v, _A_SLOT_IDX)
    zeros64 = jnp.zeros((64, A_HEADS * LANES), F32)
    d_lane = np.full(LANES, -1, np.int32)
    d_lane[D_HEADS:2 * D_HEADS] = np.arange(D_HEADS)
    wb = w["w_branch"][l]
    return {
        "norm_mix_pre": row(w["norm_mix_pre"][l]), "norm_mix_post": row(w["norm_mix_post"][l]),
        "norm_ffn_pre": row(w["norm_ffn_pre"][l]), "norm_ffn_post": row(w["norm_ffn_post"][l]),
        "w_in": _take_cols(w["w_in"][l], _P_IDX).astype(BF16),
        "w_merge_gate": w["w_merge_gate"][l].astype(BF16),
        "wb_a": _take_rows(wb[0], _A_SLOT_IDX).astype(BF16), "wb_b": wb[1].astype(BF16),
        "wb_c": _take_rows(wb[2], _A_SLOT_IDX).astype(BF16), "wb_d": wb[3].astype(BF16),
        "w_out": w["w_out"][l].astype(BF16),
        "w_ffn_up": w["w_ffn_up"][l].astype(BF16), "w_ffn_down": w["w_ffn_down"][l].astype(BF16),
        "a_mu": _take_cols(row(w["a_mu"][l]), _A_ROW_IDX),
        "a_w0": a_slot(row(w["a_w0"][l])), "a_a0": a_slot(row(w["a_a0"][l])),
        "a_w_up": jnp.concatenate([a_slot(w["a_w_up"][l]), zeros64], axis=0).astype(BF16),
        "a_a_up": jnp.concatenate([zeros64, a_slot(w["a_a_up"][l])], axis=0).astype(BF16),
        "a_g_up": a_slot(w["a_g_up"][l]).astype(BF16),
        "a_k_k": a_slot(row(w["a_k_k"][l])), "a_k_a": a_slot(row(w["a_k_a"][l])),
        "a_r_k": a_slot(row(w["a_r_k"][l])),
        "a_ln_w": a_slot(row(w["a_ln_w"][l])), "a_ln_b": a_slot(row(w["a_ln_b"][l])),
        "b_alpha_up": jnp.pad(_take_cols(w["b_alpha_up"][l], _B_SLOT_IDX), ((0, LANES - 16), (0, 0))).astype(BF16),
        "b_alpha_bias": _take_cols(row(w["b_alpha_bias"][l]), _B_SLOT_IDX),
        "b_norm": row(w["b_norm"][l]),
        "c_gtab": _band_bias_row(w["c_rel_bias"][l]),
        "d_conv_w": w["d_conv_w"][l],
        "d_a_log": _take_cols(row(w["d_a_log"][l]), d_lane), "d_dt_bias": _take_cols(row(w["d_dt_bias"][l]), d_lane),
        "d_norm": row(w["d_norm"][l]),
    }


def _layer(x, st, lw):
    bsz, t, _ = x.shape
    n = bsz * t
    shift_prev, s_a, s_b, band_k, band_v, conv_buf, s_d = st
    x2 = x.reshape(n, D_MODEL)
    u, p = _inproj(x2, lw["norm_mix_pre"], lw["w_in"])
    p3 = p.reshape(bsz, t, P_COLS)

    def seg(name, width):
        return p3[:, :, _P_OFF[name]:_P_OFF[name] + width]

    y_a, s_a_new = _rwkv(p3, _take_cols(shift_prev, _A_ROW_IDX),
                         jnp.pad(s_a, ((0, 0), (0, 0), (0, LANES - A_HD), (0, LANES - A_HD))), lw)
    s_b_t = jnp.pad(jnp.swapaxes(s_b, -1, -2), ((0, 0), (0, 0), (0, 0), (0, LANES - B_DK)))
    y_b, s_b_new = _gla(p3, s_b_t, lw)
    hw = C_HEADS * LANES
    k_new, v_new = seg("c_k", hw), seg("c_v", hw)
    if band_k is None:
        pad_rows = C_BAND
        front = jnp.zeros((bsz, C_BAND, hw), BF16)
    else:
        pad_rows = 0
        slot = lambda z: jnp.pad(z, ((0, 0), (0, 0), (0, 0), (0, LANES - C_HD))).reshape(bsz, C_BAND, hw).astype(BF16)
        front = None
    tail = jnp.zeros((bsz, CHUNK, hw), BF16)
    kpad = jnp.concatenate([front if band_k is None else slot(band_k), k_new.astype(BF16), tail], axis=1)
    vpad = jnp.concatenate([front if band_v is None else slot(band_v), v_new.astype(BF16), tail], axis=1)
    y_c = _band(p3, kpad, vpad, lw["c_gtab"], pad_rows)
    y_d, s_d_new = _dn(p3, jnp.pad(conv_buf, ((0, 0), (8 - (D_CONV - 1), 0), (0, 0))), s_d, lw)

    h = _merge(u, y_a.reshape(n, -1), y_b.reshape(n, -1), y_c.reshape(n, -1), y_d.reshape(n, -1), x2, lw)
    out = _ffn(h, lw).reshape(bsz, t, D_MODEL)

    keep = min(C_BAND, t)
    unslot = lambda z: z.reshape(bsz, t, C_HEADS, LANES)[:, t - keep:, :, :C_HD]
    new_state = (
        _take_cols(p3[:, t - 1:, :], _A_ORIG_POS),
        s_a_new[:, :, :A_HD, :A_HD],
        jnp.swapaxes(s_b_new[:, :, :, :B_DK], -1, -2),
        unslot(k_new), unslot(v_new),
        seg("d_qkv", 3 * MIX_W)[:, t - (D_CONV - 1):, :],
        s_d_new,
    )
    return out, new_state


def kernel(x_prompt, x_sample, state_rwkv_shift, state_rwkv, state_gla, cache_band_k, cache_band_v, state_dn_conv, state_dn, norm_mix_pre, norm_mix_post, norm_ffn_pre, norm_ffn_post, w_in, w_merge_gate, w_branch, w_out, w_ffn_up, w_ffn_down, a_mu, a_w0, a_w_up, a_a0, a_a_up, a_g_up, a_k_k, a_k_a, a_r_k, a_ln_w, a_ln_b, b_alpha_up, b_alpha_bias, b_norm, c_rel_bias, d_conv_w, d_a_log, d_dt_bias, d_norm):
    weights = dict(norm_mix_pre=norm_mix_pre, norm_mix_post=norm_mix_post, norm_ffn_pre=norm_ffn_pre,
                   norm_ffn_post=norm_ffn_post, w_in=w_in, w_merge_gate=w_merge_gate, w_branch=w_branch,
                   w_out=w_out, w_ffn_up=w_ffn_up, w_ffn_down=w_ffn_down, a_mu=a_mu, a_w0=a_w0, a_w_up=a_w_up,
                   a_a0=a_a0, a_a_up=a_a_up, a_g_up=a_g_up, a_k_k=a_k_k, a_k_a=a_k_a,
                   a_r_k=a_r_k.reshape(a_r_k.shape[0], -1), a_ln_w=a_ln_w, a_ln_b=a_ln_b, b_alpha_up=b_alpha_up,
                   b_alpha_bias=b_alpha_bias, b_norm=b_norm, c_rel_bias=c_rel_bias, d_conv_w=d_conv_w,
                   d_a_log=d_a_log, d_dt_bias=d_dt_bias, d_norm=d_norm)
    bsz = x_prompt.shape[0]
    depth = w_in.shape[0]
    y_p, y_s = x_prompt, x_sample
    new_p, new_s = [], []
    for l in range(depth):
        lw = _prep_layer(l, weights)
        zero_state = (jnp.zeros((bsz, 1, A_IN), F32), jnp.zeros((bsz, A_HEADS, A_HD, A_HD), F32),
                      jnp.zeros((bsz, B_HEADS, B_DK, B_DV), F32), None, None,
                      jnp.zeros((bsz, D_CONV - 1, 3 * MIX_W), F32), jnp.zeros((bsz, D_HEADS, D_HD, D_HD), F32))
        y_p, st_p = _layer(y_p, zero_state, lw)
        y_s, st_s = _layer(y_s, (state_rwkv_shift[l], state_rwkv[l], state_gla[l], cache_band_k[l],
                                 cache_band_v[l], state_dn_conv[l], state_dn[l]), lw)
        new_p.append(st_p)
        new_s.append(st_s)
    stack = lambda states, i: jnp.stack([s[i] for s in states])
    return ((y_p, y_s) + tuple(stack(new_p, i) for i in range(7)) + tuple(stack(new_s, i) for i in range(7)))
```

```python
import functools

import numpy as np
import jax
import jax.numpy as jnp
from jax import lax
from jax.experimental import pallas as pl
from jax.experimental.pallas import tpu as pltpu

F32 = jnp.float32
BF16 = jnp.bfloat16

D_MODEL = 2048
MIX_W = 512
D_FF = 4 * D_MODEL
CHUNK = 64
LANES = 128
NORM_EPS = 1e-6
VMEM_LIMIT = 56 * 1024 * 1024

A_HEADS, A_HD = 8, 64
A_DECAY_SCALE = 0.6065306597
A_GN_EPS = 64e-5
A_IN = 1792
B_HEADS, B_DK, B_DV = 4, 64, 128
B_GATE_TEMP = 16.0
B_IN = 1552
C_HEADS, C_HD = 8, 64
C_BAND = 8 * CHUNK
C_REL_CLIP = 2 * CHUNK
C_WIN = C_BAND + 2 * CHUNK
C_IN = 1536
D_HEADS, D_HD = 4, 128
D_CONV = 4
D_IN = 2056
IN_TOTAL = A_IN + B_IN + C_IN + D_IN


def _head_pad(cols, head):
    out = []
    for h in range(len(cols) // head):
        out.extend(cols[h * head:(h + 1) * head])
        out.extend([-1] * (LANES - head))
    return out


def _pad_to(cols, width):
    return list(cols) + [-1] * (width - len(cols))


def _build_layout():
    a0, b0, c0, d0 = 0, A_IN, A_IN + B_IN, A_IN + B_IN + C_IN
    rng = lambda s, n: list(range(s, s + n))
    segs = [
        ("a_r", _head_pad(rng(a0, 512), 64)), ("a_k", _head_pad(rng(a0 + 512, 512), 64)),
        ("a_v", _head_pad(rng(a0 + 1024, 512), 64)),
        ("c_q", _head_pad(rng(c0, 512), 64)), ("c_k", _head_pad(rng(c0 + 512, 512), 64)),
        ("c_v", _head_pad(rng(c0 + 1024, 512), 64)),
        ("d_qkv", rng(d0, 1536)), ("d_gate", rng(d0 + 1544, 512)),
        ("b_q", _head_pad(rng(b0, 256), 64)), ("b_k", _head_pad(rng(b0 + 256, 256), 64)),
        ("b_v", rng(b0 + 512, 512)), ("b_rg", rng(b0 + 1040, 512)),
        ("a_wa", rng(a0 + 1536, 128)), ("a_gl", rng(a0 + 1664, 128)),
        ("b_al", _pad_to(rng(b0 + 1024, 16), LANES)), ("d_ba", _pad_to(rng(d0 + 1536, 8), LANES)),
    ]
    idx, off = [], {}
    for name, cols in segs:
        off[name] = len(idx)
        idx.extend(cols)
    return np.asarray(idx, np.int32), off


_P_IDX, _P_OFF = _build_layout()
P_COLS = int(_P_IDX.shape[0])
_A_SLOT_IDX = np.asarray(_head_pad(list(range(512)), 64), np.int32)
_B_SLOT_IDX = np.asarray(_head_pad(list(range(256)), 64), np.int32)
_A_ROW_IDX = np.concatenate([_A_SLOT_IDX, np.where(_A_SLOT_IDX >= 0, _A_SLOT_IDX + 512, -1),
                             np.where(_A_SLOT_IDX >= 0, _A_SLOT_IDX + 1024, -1),
                             np.arange(1536, 1792, dtype=np.int32)]).astype(np.int32)
A_ROW = int(_A_ROW_IDX.shape[0])


def _take_static(x, idx, axis):
    idx = np.asarray(idx)
    axis = axis % x.ndim
    pieces, i = [], 0
    while i < len(idx):
        j = i + 1
        if idx[i] < 0:
            while j < len(idx) and idx[j] < 0:
                j += 1
            pieces.append(jnp.zeros(x.shape[:axis] + (j - i,) + x.shape[axis + 1:], x.dtype))
        else:
            while j < len(idx) and idx[j] == idx[j - 1] + 1:
                j += 1
            pieces.append(lax.slice_in_dim(x, int(idx[i]), int(idx[i]) + (j - i), axis=axis))
        i = j
    return pieces[0] if len(pieces) == 1 else jnp.concatenate(pieces, axis=axis)


def _take_cols(x, idx):
    return _take_static(x, idx, -1)


def _take_rows(x, idx):
    return _take_static(x, idx, -2)


def _a_row_to_orig():
    pos = np.zeros(A_IN, np.int32)
    base = {0: _P_OFF["a_r"], 1: _P_OFF["a_k"], 2: _P_OFF["a_v"]}
    for i in range(1536):
        part, j = divmod(i, 512)
        pos[i] = base[part] + (j // 64) * LANES + (j % 64)
    pos[1536:1664] = _P_OFF["a_wa"] + np.arange(128)
    pos[1664:1792] = _P_OFF["a_gl"] + np.arange(128)
    return pos


_A_ORIG_POS = _a_row_to_orig()


def _mm(a, b):
    return jnp.dot(a.astype(BF16), b.astype(BF16), preferred_element_type=F32)


def _mm_nt(a, b):
    return lax.dot_general(a.astype(BF16), b.astype(BF16), (((1,), (1,)), ((), ())),
                           preferred_element_type=F32)


def _split(x):
    hi = x.astype(BF16)
    lo = (x - hi.astype(F32)).astype(BF16)
    return hi, lo


def _mm3(a, b):
    ah, al = _split(a)
    bh, bl = _split(b)
    return (jnp.dot(ah, bh, preferred_element_type=F32) + jnp.dot(ah, bl, preferred_element_type=F32)
            + jnp.dot(al, bh, preferred_element_type=F32))


def _mm_exact_lhs(a_bf16, b):
    bh, bl = _split(b)
    return jnp.dot(a_bf16, bh, preferred_element_type=F32) + jnp.dot(a_bf16, bl, preferred_element_type=F32)


def _iota2(shape, axis):
    return lax.broadcasted_iota(jnp.int32, shape, axis)


def _tri_inverse(a):
    n = a.shape[0]
    eye = (_iota2((n, n), 0) == _iota2((n, n), 1)).astype(F32)
    t = eye + a
    x = a
    for _ in range(int(np.log2(n)) - 1):
        x = _mm3(x, x)
        t = t + _mm3(t, x)
    return t


def _tri_inverse_many(mats):
    n = mats[0].shape[0]
    eye = (_iota2((n, n), 0) == _iota2((n, n), 1)).astype(F32)
    ts = [eye + a for a in mats]
    xs = list(mats)
    for _ in range(int(np.log2(n)) - 1):
        xs = [_mm3(x, x) for x in xs]
        ts = [t + _mm3(t, x) for t, x in zip(ts, xs)]
    return ts


def _sigmoid(x):
    return 1.0 / (1.0 + jnp.exp(-x))


def _softplus(x):
    return jnp.maximum(x, 0.0) + jnp.log(1.0 + jnp.exp(-jnp.abs(x)))


def _rms(x, gain):
    return x * lax.rsqrt(jnp.mean(x * x, axis=-1, keepdims=True) + NORM_EPS) * gain


def _lower_tri_bf16(n):
    return (_iota2((n, n), 0) >= _iota2((n, n), 1)).astype(BF16)


def _shift_rows(x, s, carry_rows):
    rows = _iota2(x.shape, 0)
    out = pltpu.roll(x, s, 0)
    for t in range(s):
        out = jnp.where(rows == t, jnp.broadcast_to(carry_rows[t], x.shape), out)
    return out


def _inproj_kernel(x_ref, g_ref, w_ref, u_ref, p_ref, u_scr):
    @pl.when(pl.program_id(1) == 0)
    def _():
        u = _rms(x_ref[...], g_ref[...]).astype(BF16)
        u_scr[...] = u
        u_ref[...] = u

    p_ref[...] = jnp.dot(u_scr[...], w_ref[...], preferred_element_type=F32)


def _inproj(x2, gain, w):
    n = x2.shape[0]
    tm = min(n, 1024)
    tn = 512
    return pl.pallas_call(
        _inproj_kernel,
        grid=(n // tm, P_COLS // tn),
        in_specs=[pl.BlockSpec((tm, D_MODEL), lambda i, j: (i, 0)),
                  pl.BlockSpec((1, D_MODEL), lambda i, j: (0, 0)),
                  pl.BlockSpec((D_MODEL, tn), lambda i, j: (0, j))],
        out_specs=[pl.BlockSpec((tm, D_MODEL), lambda i, j: (i, 0)),
                   pl.BlockSpec((tm, tn), lambda i, j: (i, j))],
        out_shape=[jax.ShapeDtypeStruct((n, D_MODEL), BF16), jax.ShapeDtypeStruct((n, P_COLS), F32)],
        scratch_shapes=[pltpu.VMEM((tm, D_MODEL), BF16)],
        compiler_params=pltpu.CompilerParams(dimension_semantics=("parallel", "arbitrary"),
                                             vmem_limit_bytes=VMEM_LIMIT),
        name="inproj",
    )(x2, gain, w)


def _rwkv_kernel(pr_ref, pk_ref, pv_ref, pwa_ref, pgl_ref, sh0_ref, s0_ref, mu_ref, w0_ref, wup_ref,
                 a0_ref, aup_ref, gup_ref, kk_ref, ka_ref, rk_ref, lnw_ref, lnb_ref,
                 y_ref, s_scr, carry_scr):
    c = pl.program_id(1)
    C = CHUNK
    HW = A_HEADS * LANES

    @pl.when(c == 0)
    def _():
        carry_scr[...] = sh0_ref[...]
        s_scr[...] = s0_ref[...]

    def token_shift(p_ref, lo, hi):
        p = p_ref[...]
        prev = _shift_rows(p, 1, [carry_scr[:, lo:hi]])
        carry_scr[:, lo:hi] = p[C - 1:C, :]
        return p + (prev - p) * mu_ref[:, lo:hi]

    xr = token_shift(pr_ref, 0, HW)
    xk = token_shift(pk_ref, HW, 2 * HW)
    xv = token_shift(pv_ref, 2 * HW, 3 * HW)
    xwa = token_shift(pwa_ref, 3 * HW, 3 * HW + LANES)
    xgl = token_shift(pgl_ref, 3 * HW + LANES, 3 * HW + 2 * LANES)

    logw = -A_DECAY_SCALE * _sigmoid(w0_ref[...] + _mm(jnp.tanh(xwa), wup_ref[...]))
    a = _sigmoid(a0_ref[...] + _mm(xwa, aup_ref[...]))
    g = _mm(_sigmoid(xgl), gup_ref[...])
    kkv = xk * kk_ref[...]
    kmod = xk * (1.0 + (a - 1.0) * ka_ref[...])
    cum = _mm_exact_lhs(_lower_tri_bf16(C), logw)

    ri, ci = _iota2((2 * C, C), 0), _iota2((2 * C, C), 1)
    mask2 = jnp.logical_or(ri % C > ci, jnp.logical_and(ri >= C, ri - C == ci))
    real = _iota2((C, LANES), 1) < A_HD
    heads = range(A_HEADS)
    sls = [slice(h * LANES, (h + 1) * LANES) for h in heads]

    lhs2, bhs, khs, bks, vs = [], [], [], [], []
    for sl in sls:
        kk_h = kkv[:, sl]
        kk_h = kk_h * lax.rsqrt(jnp.sum(kk_h * kk_h, axis=-1, keepdims=True) + 1e-6)
        cum_h = cum[:, sl]
        g_incl = jnp.exp(cum_h)
        g_inv = jnp.exp(-cum_h)
        g_last = g_incl[C - 1:C, :]
        at = kk_h * jnp.exp(cum_h - logw[:, sl])
        rt = xr[:, sl] * g_incl
        bh = -(kk_h * a[:, sl]) * g_inv
        kh = kmod[:, sl] * g_inv
        lhs2.append(jnp.concatenate([at, rt], axis=0).astype(BF16))
        bhs.append(bh.astype(BF16))
        khs.append(kh.astype(BF16))
        bks.append(jnp.concatenate([bh * g_last, kh * g_last], axis=0).astype(BF16))
        vs.append(xv[:, sl])
    s_old = [s_scr[h] for h in heads]
    aab = [jnp.where(mask2, _mm_nt(lhs2[h], bhs[h]), 0.0) for h in heads]
    aak = [jnp.where(mask2, _mm_nt(lhs2[h], khs[h]), 0.0) for h in heads]
    p0 = [_mm_nt(lhs2[h], s_old[h]) for h in heads]
    t_inv = _tri_inverse_many([aab[h][0:C] for h in heads])
    rhs_u = [p0[h][0:C] + _mm(aak[h][0:C], vs[h]) for h in heads]
    us = [_mm(t_inv[h], rhs_u[h]) for h in heads]
    ys = [p0[h][C:2 * C] + _mm(aab[h][C:2 * C], us[h]) + _mm(aak[h][C:2 * C], vs[h]) for h in heads]
    for h in heads:
        z = jnp.concatenate([us[h], vs[h]], axis=0)
        g_last = jnp.exp(cum[C - 1:C, sls[h]])
        s_scr[h] = s_old[h] * g_last + _mm(z.T, bks[h])
    for h, sl in zip(heads, sls):
        y = ys[h]
        mean = jnp.sum(y, axis=-1, keepdims=True) * (1.0 / A_HD)
        d = jnp.where(real, y - mean, 0.0)
        var = jnp.sum(d * d, axis=-1, keepdims=True) * (1.0 / A_HD)
        yn = d * lax.rsqrt(var + A_GN_EPS) * lnw_ref[:, sl] + lnb_ref[:, sl]
        bonus = jnp.sum(xr[:, sl] * kmod[:, sl] * rk_ref[:, sl], axis=-1, keepdims=True) * vs[h]
        y_ref[:, sl] = ((yn + bonus) * g[:, sl]).astype(BF16)


def _rwkv(p3, shift_prev, s0, lw):
    bsz, t, _ = p3.shape
    hw = A_HEADS * LANES
    blk = lambda w, j: pl.BlockSpec((None, CHUNK, w), lambda b, c: (b, c, j))
    full = lambda shp: pl.BlockSpec(shp, lambda b, c: (0,) * len(shp))
    return pl.pallas_call(
        _rwkv_kernel,
        grid=(bsz, t // CHUNK),
        in_specs=[blk(hw, _P_OFF["a_r"] // hw), blk(hw, _P_OFF["a_k"] // hw), blk(hw, _P_OFF["a_v"] // hw),
                  blk(LANES, _P_OFF["a_wa"] // LANES), blk(LANES, _P_OFF["a_gl"] // LANES),
                  pl.BlockSpec((None, 1, A_ROW), lambda b, c: (b, 0, 0)),
                  pl.BlockSpec((None, A_HEADS, LANES, LANES), lambda b, c: (b, 0, 0, 0)),
                  full((1, A_ROW)), full((1, hw)), full((LANES, hw)), full((1, hw)), full((LANES, hw)),
                  full((LANES, hw)), full((1, hw)), full((1, hw)), full((1, hw)), full((1, hw)), full((1, hw))],
        out_specs=[pl.BlockSpec((None, CHUNK, hw), lambda b, c: (b, c, 0)),
                   pl.BlockSpec((None, A_HEADS, LANES, LANES), lambda b, c: (b, 0, 0, 0))],
        out_shape=[jax.ShapeDtypeStruct((bsz, t, hw), BF16),
                   jax.ShapeDtypeStruct((bsz, A_HEADS, LANES, LANES), F32)],
        scratch_shapes=[pltpu.VMEM((1, A_ROW), F32)],
        compiler_params=pltpu.CompilerParams(dimension_semantics=("parallel", "arbitrary"),
                                             vmem_limit_bytes=VMEM_LIMIT),
        name="rwkv7",
    )(p3, p3, p3, p3, p3, shift_prev, s0, lw["a_mu"], lw["a_w0"], lw["a_w_up"], lw["a_a0"], lw["a_a_up"],
      lw["a_g_up"], lw["a_k_k"], lw["a_k_a"], lw["a_r_k"], lw["a_ln_w"], lw["a_ln_b"])


def _gla_kernel(q_ref, k_ref, v_ref, rg_ref, al_ref, s0_ref, aup_ref, ab_ref, nw_ref,
                y_ref, s_scr):
    c = pl.program_id(1)
    C = CHUNK

    @pl.when(c == 0)
    def _():
        s_scr[...] = s0_ref[...]

    x = _mm(al_ref[...], aup_ref[...]) + ab_ref[...]
    loga = -_softplus(-x) * (1.0 / B_GATE_TEMP)
    cum = _mm_exact_lhs(_lower_tri_bf16(C), loga)
    causal = _iota2((C, C), 0) >= _iota2((C, C), 1)
    q, k, v, rg = q_ref[...], k_ref[...], v_ref[...], rg_ref[...]

    heads = range(B_HEADS)
    sls = [slice(h * LANES, (h + 1) * LANES) for h in heads]
    qts, kts, kds, vhs, elast = [], [], [], [], []
    for sl in sls:
        b = cum[:, sl]
        b_last = b[C - 1:C, :]
        qts.append((q[:, sl] * (B_DK ** -0.5) * jnp.exp(b)).astype(BF16))
        kts.append((k[:, sl] * jnp.exp(-b)).astype(BF16))
        kds.append((k[:, sl] * jnp.exp(b_last - b)).astype(BF16))
        vhs.append(v[:, sl].astype(BF16))
        elast.append(jnp.exp(b_last))
    s_old = [s_scr[h] for h in heads]
    atts = [jnp.where(causal, _mm_nt(qts[h], kts[h]), 0.0) for h in heads]
    inter = [_mm_nt(qts[h], s_old[h]) for h in heads]
    outs = [inter[h] + _mm(atts[h], vhs[h]) for h in heads]
    zeros = jnp.zeros((C, LANES), BF16)
    for h in heads:
        vt = jnp.concatenate([v[:, sls[h]], jnp.zeros((C, LANES), F32)], axis=0).T
        s_scr[h] = s_old[h] * elast[h] + _mm(vt, jnp.concatenate([kds[h], zeros], axis=0))
    for h, sl in zip(heads, sls):
        o = outs[h]
        o = o * lax.rsqrt(jnp.mean(o * o, axis=-1, keepdims=True) + NORM_EPS)
        rg_h = rg[:, sl]
        y_ref[:, sl] = (o * nw_ref[:, sl] * (rg_h * _sigmoid(rg_h))).astype(BF16)


def _gla(p3, s0, lw):
    bsz, t, _ = p3.shape
    blk = lambda w, j: pl.BlockSpec((None, CHUNK, w), lambda b, c: (b, c, j))
    full = lambda shp: pl.BlockSpec(shp, lambda b, c: (0,) * len(shp))
    st = pl.BlockSpec((None, B_HEADS, LANES, LANES), lambda b, c: (b, 0, 0, 0))
    return pl.pallas_call(
        _gla_kernel,
        grid=(bsz, t // CHUNK),
        in_specs=[blk(MIX_W, _P_OFF["b_q"] // MIX_W), blk(MIX_W, _P_OFF["b_k"] // MIX_W),
                  blk(MIX_W, _P_OFF["b_v"] // MIX_W), blk(MIX_W, _P_OFF["b_rg"] // MIX_W),
                  blk(LANES, _P_OFF["b_al"] // LANES), st,
                  full((LANES, MIX_W)), full((1, MIX_W)), full((1, MIX_W))],
        out_specs=[pl.BlockSpec((None, CHUNK, MIX_W), lambda b, c: (b, c, 0)), st],
        out_shape=[jax.ShapeDtypeStruct((bsz, t, MIX_W), BF16),
                   jax.ShapeDtypeStruct((bsz, B_HEADS, LANES, LANES), F32)],
        compiler_params=pltpu.CompilerParams(dimension_semantics=("parallel", "arbitrary"),
                                             vmem_limit_bytes=VMEM_LIMIT),
        name="gla",
    )(p3, p3, p3, p3, p3, s0, lw["b_alpha_up"], lw["b_alpha_bias"], lw["b_norm"])


def _band_kernel(q_ref, k_ref, v_ref, g_ref, y_ref, bias_scr, *, pad_rows):
    b = pl.program_id(0)
    c = pl.program_id(1)
    C = CHUNK

    @pl.when(jnp.logical_and(b == 0, c == 0))
    def _():
        rows = _iota2((C, C_WIN), 0)
        for h in range(C_HEADS):
            x = jnp.broadcast_to(g_ref[h:h + 1, :], (C, C_WIN))
            for bit in range(6):
                x = jnp.where(((rows >> bit) & 1) == 1, pltpu.roll(x, 1 << bit, 1), x)
            bias_scr[h] = x

    start = pl.multiple_of(c * C, C)
    col = _iota2((C, C_WIN), 1)
    valid = jnp.logical_and(col < C_BAND + C, col + c * C >= pad_rows)
    heads = range(C_HEADS)
    sls = [slice(h * LANES, (h + 1) * LANES) for h in heads]
    scores = [_mm_nt((q_ref[:, sl] * (C_HD ** -0.5)).astype(BF16), k_ref[pl.ds(start, C_WIN), sl]) for sl in sls]
    probs, denoms = [], []
    for h in heads:
        s = jnp.where(valid, scores[h] + bias_scr[h], -jnp.inf)
        p = jnp.exp(s - jnp.max(s, axis=-1, keepdims=True))
        denoms.append(jnp.sum(p, axis=-1, keepdims=True))
        probs.append(p.astype(BF16))
    outs = [_mm(probs[h], v_ref[pl.ds(start, C_WIN), sls[h]]) for h in heads]
    for h, sl in zip(heads, sls):
        y_ref[:, sl] = (outs[h] / denoms[h]).astype(BF16)


def _band(p3, kpad, vpad, gtab, pad_rows):
    bsz, t, _ = p3.shape
    hw = C_HEADS * LANES
    tp = kpad.shape[1]
    kv = pl.BlockSpec((None, tp, hw), lambda b, c: (b, 0, 0))
    return pl.pallas_call(
        functools.partial(_band_kernel, pad_rows=pad_rows),
        grid=(bsz, t // CHUNK),
        in_specs=[pl.BlockSpec((None, CHUNK, hw), lambda b, c: (b, c, _P_OFF["c_q"] // hw)), kv, kv,
                  pl.BlockSpec((C_HEADS, C_WIN), lambda b, c: (0, 0))],
        out_specs=pl.BlockSpec((None, CHUNK, hw), lambda b, c: (b, c, 0)),
        out_shape=jax.ShapeDtypeStruct((bsz, t, hw), BF16),
        scratch_shapes=[pltpu.VMEM((C_HEADS, CHUNK, C_WIN), F32)],
        compiler_params=pltpu.CompilerParams(dimension_semantics=("arbitrary", "arbitrary"),
                                             vmem_limit_bytes=VMEM_LIMIT),
        name="band_attn",
    )(p3, kpad, vpad, gtab)


def _dn_kernel(qkv_ref, ba_ref, gate_ref, cb0_ref, s0_ref, cw_ref, alog_ref, dtb_ref, nw_ref,
               y_ref, s_scr, carry_scr):
    c = pl.program_id(1)
    C = CHUNK

    @pl.when(c == 0)
    def _():
        carry_scr[...] = cb0_ref[...]
        s_scr[...] = s0_ref[...]

    x = qkv_ref[...]
    c0, c1, c2 = carry_scr[5:6, :], carry_scr[6:7, :], carry_scr[7:8, :]
    conv = (_shift_rows(x, 3, [c0, c1, c2]) * cw_ref[0:1, :] + _shift_rows(x, 2, [c1, c2]) * cw_ref[1:2, :]
            + _shift_rows(x, 1, [c2]) * cw_ref[2:3, :] + x * cw_ref[3:4, :])
    carry_scr[...] = x[C - 8:C, :]
    qkv = conv * _sigmoid(conv)

    ba = ba_ref[...]
    beta_all = _sigmoid(ba)
    g_all = -jnp.exp(alog_ref[...]) * _softplus(ba + dtb_ref[...])
    tri = _lower_tri_bf16(C)
    cum_all = _mm_exact_lhs(tri, g_all)
    lane = _iota2((C, LANES), 1)
    col_of = lambda arr, j: jnp.sum(jnp.where(lane == j, arr, 0.0), axis=-1, keepdims=True)
    ri, ci = _iota2((C, C), 0), _iota2((C, C), 1)
    after = (ri > ci).astype(F32)
    gate = gate_ref[...]

    heads = range(D_HEADS)
    sls = [slice(h * LANES, (h + 1) * LANES) for h in heads]
    betas = [col_of(beta_all, h) for h in heads]
    cums = [col_of(cum_all, D_HEADS + h) for h in heads]
    diffs = [_mm_exact_lhs(tri, col_of(g_all, D_HEADS + h) * after) for h in heads]
    decays = [jnp.where(ri >= ci, jnp.exp(diffs[h]), 0.0) for h in heads]
    qs, ks, vs = [], [], []
    for h in heads:
        q_h = qkv[:, h * LANES:(h + 1) * LANES]
        k_h = qkv[:, MIX_W + h * LANES:MIX_W + (h + 1) * LANES]
        qs.append(q_h * lax.rsqrt(jnp.sum(q_h * q_h, axis=-1, keepdims=True) + 1e-6) * (D_HD ** -0.5))
        ks.append(k_h * lax.rsqrt(jnp.sum(k_h * k_h, axis=-1, keepdims=True) + 1e-6))
        vs.append(qkv[:, 2 * MIX_W + h * LANES:2 * MIX_W + (h + 1) * LANES])
    kbf = [k.astype(BF16) for k in ks]
    kks = [_mm_nt(kbf[h], kbf[h]) for h in heads]
    atts = [_mm_nt(qs[h], kbf[h]) * decays[h] for h in heads]
    t_inv = _tri_inverse_many([jnp.where(ri > ci, -(betas[h] * decays[h] * kks[h]), 0.0) for h in heads])
    ecums = [jnp.exp(cums[h]) for h in heads]
    uws = [_mm(t_inv[h], jnp.concatenate([vs[h] * betas[h], ks[h] * (betas[h] * ecums[h])], axis=1)) for h in heads]
    s_old = [s_scr[h] for h in heads]
    deltas = [uws[h][:, 0:LANES] - _mm(uws[h][:, LANES:2 * LANES], s_old[h]) for h in heads]
    outs = [_mm(qs[h] * ecums[h], s_old[h]) + _mm(atts[h], deltas[h]) for h in heads]
    zeros = jnp.zeros((C, LANES), F32)
    for h in heads:
        cum_last = cums[h][C - 1:C, :]
        kdt = jnp.concatenate([ks[h] * jnp.exp(cum_last - cums[h]), zeros], axis=0).T
        s_scr[h] = s_old[h] * jnp.exp(cum_last) + _mm(kdt, jnp.concatenate([deltas[h], zeros], axis=0))
    for h, sl in zip(heads, sls):
        o = outs[h]
        o = o * lax.rsqrt(jnp.mean(o * o, axis=-1, keepdims=True) + NORM_EPS) * nw_ref[...]
        gt = gate[:, sl]
        y_ref[:, sl] = (o * (gt * _sigmoid(gt))).astype(BF16)


def _dn(p3, conv_buf, s0, lw):
    bsz, t, _ = p3.shape
    w3 = 3 * MIX_W
    full = lambda shp: pl.BlockSpec(shp, lambda b, c: (0,) * len(shp))
    st = pl.BlockSpec((None, D_HEADS, LANES, LANES), lambda b, c: (b, 0, 0, 0))
    return pl.pallas_call(
        _dn_kernel,
        grid=(bsz, t // CHUNK),
        in_specs=[pl.BlockSpec((None, CHUNK, w3), lambda b, c: (b, c, _P_OFF["d_qkv"] // w3)),
                  pl.BlockSpec((None, CHUNK, LANES), lambda b, c: (b, c, _P_OFF["d_ba"] // LANES)),
                  pl.BlockSpec((None, CHUNK, MIX_W), lambda b, c: (b, c, _P_OFF["d_gate"] // MIX_W)),
                  pl.BlockSpec((None, 8, w3), lambda b, c: (b, 0, 0)), st,
                  full((D_CONV, w3)), full((1, LANES)), full((1, LANES)), full((1, LANES))],
        out_specs=[pl.BlockSpec((None, CHUNK, MIX_W), lambda b, c: (b, c, 0)), st],
        out_shape=[jax.ShapeDtypeStruct((bsz, t, MIX_W), BF16),
                   jax.ShapeDtypeStruct((bsz, D_HEADS, LANES, LANES), F32)],
        scratch_shapes=[pltpu.VMEM((8, w3), F32)],
        compiler_params=pltpu.CompilerParams(dimension_semantics=("parallel", "arbitrary"),
                                             vmem_limit_bytes=VMEM_LIMIT),
        name="deltanet",
    )(p3, p3, p3, conv_buf, s0, lw["d_conv_w"], lw["d_a_log"], lw["d_dt_bias"], lw["d_norm"])


def _merge_kernel(u_ref, ya_ref, yb_ref, yc_ref, yd_ref, x_ref, wg_ref, wba_ref, wbb_ref, wbc_ref, wbd_ref,
                  wo_ref, gain_ref, h_ref, acc_ref):
    j = pl.program_id(1)

    @pl.when(j == 0)
    def _():
        acc_ref[...] = jnp.zeros_like(acc_ref)

    u = u_ref[...]
    merged = None
    for b, (y_r, wb_r) in enumerate(((ya_ref, wba_ref), (yb_ref, wbb_ref), (yc_ref, wbc_ref), (yd_ref, wbd_ref))):
        gate = _sigmoid(jnp.dot(u, wg_ref[b], preferred_element_type=F32))
        term = gate * jnp.dot(y_r[...], wb_r[...], preferred_element_type=F32)
        merged = term if merged is None else merged + term
    acc_ref[...] += jnp.dot(merged.astype(BF16), wo_ref[...], preferred_element_type=F32)

    @pl.when(j == pl.num_programs(1) - 1)
    def _():
        h_ref[...] = x_ref[...] + _rms(acc_ref[...], gain_ref[...])


def _merge(u, ya, yb, yc, yd, x2, lw):
    n = x2.shape[0]
    tm = min(n, 512)
    tn = 256
    row = lambda w: pl.BlockSpec((tm, w), lambda i, j: (i, 0))
    wcol = lambda k: pl.BlockSpec((k, tn), lambda i, j: (0, j))
    return pl.pallas_call(
        _merge_kernel,
        grid=(n // tm, D_MODEL // tn),
        in_specs=[row(D_MODEL), row(ya.shape[1]), row(yb.shape[1]), row(yc.shape[1]), row(yd.shape[1]),
                  row(D_MODEL),
                  pl.BlockSpec((4, D_MODEL, tn), lambda i, j: (0, 0, j)),
                  wcol(ya.shape[1]), wcol(yb.shape[1]), wcol(yc.shape[1]), wcol(yd.shape[1]),
                  pl.BlockSpec((tn, D_MODEL), lambda i, j: (j, 0)),
                  pl.BlockSpec((1, D_MODEL), lambda i, j: (0, 0))],
        out_specs=row(D_MODEL),
        out_shape=jax.ShapeDtypeStruct((n, D_MODEL), F32),
        scratch_shapes=[pltpu.VMEM((tm, D_MODEL), F32)],
        compiler_params=pltpu.CompilerParams(dimension_semantics=("parallel", "arbitrary"),
                                             vmem_limit_bytes=VMEM_LIMIT),
        name="merge",
    )(u, ya, yb, yc, yd, x2, lw["w_merge_gate"], lw["wb_a"], lw["wb_b"], lw["wb_c"], lw["wb_d"],
      lw["w_out"], lw["norm_mix_post"])


def _ffn_kernel(h_ref, g1_ref, wu_ref, wd_ref, g2_ref, o_ref, z_scr, acc_ref):
    j = pl.program_id(1)

    @pl.when(j == 0)
    def _():
        z_scr[...] = _rms(h_ref[...], g1_ref[...]).astype(BF16)
        acc_ref[...] = jnp.zeros_like(acc_ref)

    a = jnp.maximum(jnp.dot(z_scr[...], wu_ref[...], preferred_element_type=F32), 0.0)
    acc_ref[...] += jnp.dot((a * a).astype(BF16), wd_ref[...], preferred_element_type=F32)

    @pl.when(j == pl.num_programs(1) - 1)
    def _():
        o_ref[...] = h_ref[...] + _rms(acc_ref[...], g2_ref[...])


def _ffn(h, lw):
    n = h.shape[0]
    tm = min(n, 512)
    tf = 1024
    vec = pl.BlockSpec((1, D_MODEL), lambda i, j: (0, 0))
    return pl.pallas_call(
        _ffn_kernel,
        grid=(n // tm, D_FF // tf),
        in_specs=[pl.BlockSpec((tm, D_MODEL), lambda i, j: (i, 0)), vec,
                  pl.BlockSpec((D_MODEL, tf), lambda i, j: (0, j)),
                  pl.BlockSpec((tf, D_MODEL), lambda i, j: (j, 0)), vec],
        out_specs=pl.BlockSpec((tm, D_MODEL), lambda i, j: (i, 0)),
        out_shape=jax.ShapeDtypeStruct((n, D_MODEL), F32),
        scratch_shapes=[pltpu.VMEM((tm, D_MODEL), BF16), pltpu.VMEM((tm, D_MODEL), F32)],
        compiler_params=pltpu.CompilerParams(dimension_semantics=("parallel", "arbitrary"),
                                             vmem_limit_bytes=VMEM_LIMIT),
        name="ffn",
    )(h, lw["norm_ffn_pre"], lw["w_ffn_up"], lw["w_ffn_down"], lw["norm_ffn_post"])


def _band_bias_row(rel):
    m = np.arange(C_WIN)
    idx = np.where(m <= C_BAND + CHUNK, np.clip(C_BAND - m, -C_REL_CLIP, C_REL_CLIP) + C_REL_CLIP, 2 * C_REL_CLIP)
    return rel[:, idx.astype(np.int32)]


def _prep_layer(l, w):
    row = lambda v: v.reshape(1, -1)
    a_slot = lambda v: _take_cols(v, _A_SLOT_IDX)
    zeros64 = jnp.zeros((64, A_HEADS * LANES), F32)
    d_lane = np.full(LANES, -1, np.int32)
    d_lane[D_HEADS:2 * D_HEADS] = np.arange(D_HEADS)
    wb = w["w_branch"][l]
    return {
        "norm_mix_pre": row(w["norm_mix_pre"][l]), "norm_mix_post": row(w["norm_mix_post"][l]),
        "norm_ffn_pre": row(w["norm_ffn_pre"][l]), "norm_ffn_post": row(w["norm_ffn_post"][l]),
        "w_in": _take_cols(w["w_in"][l], _P_IDX).astype(BF16),
        "w_merge_gate": w["w_merge_gate"][l].astype(BF16),
        "wb_a": _take_rows(wb[0], _A_SLOT_IDX).astype(BF16), "wb_b": wb[1].astype(BF16),
        "wb_c": _take_rows(wb[2], _A_SLOT_IDX).astype(BF16), "wb_d": wb[3].astype(BF16),
        "w_out": w["w_out"][l].astype(BF16),
        "w_ffn_up": w["w_ffn_up"][l].astype(BF16), "w_ffn_down": w["w_ffn_down"][l].astype(BF16),
        "a_mu": _take_cols(row(w["a_mu"][l]), _A_ROW_IDX),
        "a_w0": a_slot(row(w["a_w0"][l])), "a_a0": a_slot(row(w["a_a0"][l])),
        "a_w_up": jnp.concatenate([a_slot(w["a_w_up"][l]), zeros64], axis=0).astype(BF16),
        "a_a_up": jnp.concatenate([zeros64, a_slot(w["a_a_up"][l])], axis=0).astype(BF16),
        "a_g_up": a_slot(w["a_g_up"][l]).astype(BF16),
        "a_k_k": a_slot(row(w["a_k_k"][l])), "a_k_a": a_slot(row(w["a_k_a"][l])),
        "a_r_k": a_slot(row(w["a_r_k"][l])),
        "a_ln_w": a_slot(row(w["a_ln_w"][l])), "a_ln_b": a_slot(row(w["a_ln_b"][l])),
        "b_alpha_up": jnp.pad(_take_cols(w["b_alpha_up"][l], _B_SLOT_IDX), ((0, LANES - 16), (0, 0))).astype(BF16),
        "b_alpha_bias": _take_cols(row(w["b_alpha_bias"][l]), _B_SLOT_IDX),
        "b_norm": row(w["b_norm"][l]),
        "c_gtab": _band_bias_row(w["c_rel_bias"][l]),
        "d_conv_w": w["d_conv_w"][l],
        "d_a_log": _take_cols(row(w["d_a_log"][l]), d_lane), "d_dt_bias": _take_cols(row(w["d_dt_bias"][l]), d_lane),
        "d_norm": row(w["d_norm"][l]),
    }


def _layer(x, st, lw):
    bsz, t, _ = x.shape
    n = bsz * t
    shift_prev, s_a, s_b, band_k, band_v, conv_buf, s_d = st
    x2 = x.reshape(n, D_MODEL)
    u, p = _inproj(x2, lw["norm_mix_pre"], lw["w_in"])
    p3 = p.reshape(bsz, t, P_COLS)

    def seg(name, width):
        return p3[:, :, _P_OFF[name]:_P_OFF[name] + width]

    y_a, s_a_new = _rwkv(p3, _take_cols(shift_prev, _A_ROW_IDX),
                         jnp.pad(s_a, ((0, 0), (0, 0), (0, LANES - A_HD), (0, LANES - A_HD))), lw)
    s_b_t = jnp.pad(jnp.swapaxes(s_b, -1, -2), ((0, 0), (0, 0), (0, 0), (0, LANES - B_DK)))
    y_b, s_b_new = _gla(p3, s_b_t, lw)
    hw = C_HEADS * LANES
    k_new, v_new = seg("c_k", hw), seg("c_v", hw)
    if band_k is None:
        pad_rows = C_BAND
        front = jnp.zeros((bsz, C_BAND, hw), BF16)
    else:
        pad_rows = 0
        slot = lambda z: jnp.pad(z, ((0, 0), (0, 0), (0, 0), (0, LANES - C_HD))).reshape(bsz, C_BAND, hw).astype(BF16)
        front = None
    tail = jnp.zeros((bsz, CHUNK, hw), BF16)
    kpad = jnp.concatenate([front if band_k is None else slot(band_k), k_new.astype(BF16), tail], axis=1)
    vpad = jnp.concatenate([front if band_v is None else slot(band_v), v_new.astype(BF16), tail], axis=1)
    y_c = _band(p3, kpad, vpad, lw["c_gtab"], pad_rows)
    y_d, s_d_new = _dn(p3, jnp.pad(conv_buf, ((0, 0), (8 - (D_CONV - 1), 0), (0, 0))), s_d, lw)

    h = _merge(u, y_a.reshape(n, -1), y_b.reshape(n, -1), y_c.reshape(n, -1), y_d.reshape(n, -1), x2, lw)
    out = _ffn(h, lw).reshape(bsz, t, D_MODEL)

    keep = min(C_BAND, t)
    unslot = lambda z: z.reshape(bsz, t, C_HEADS, LANES)[:, t - keep:, :, :C_HD]
    new_state = (
        _take_cols(p3[:, t - 1:, :], _A_ORIG_POS),
        s_a_new[:, :, :A_HD, :A_HD],
        jnp.swapaxes(s_b_new[:, :, :, :B_DK], -1, -2),
        unslot(k_new), unslot(v_new),
        seg("d_qkv", 3 * MIX_W)[:, t - (D_CONV - 1):, :],
        s_d_new,
    )
    return out, new_state


def kernel(x_prompt, x_sample, state_rwkv_shift, state_rwkv, state_gla, cache_band_k, cache_band_v, state_dn_conv, state_dn, norm_mix_pre, norm_mix_post, norm_ffn_pre, norm_ffn_post, w_in, w_merge_gate, w_branch, w_out, w_ffn_up, w_ffn_down, a_mu, a_w0, a_w_up, a_a0, a_a_up, a_g_up, a_k_k, a_k_a, a_r_k, a_ln_w, a_ln_b, b_alpha_up, b_alpha_bias, b_norm, c_rel_bias, d_conv_w, d_a_log, d_dt_bias, d_norm):
    weights = dict(norm_mix_pre=norm_mix_pre, norm_mix_post=norm_mix_post, norm_ffn_pre=norm_ffn_pre,
                   norm_ffn_post=norm_ffn_post, w_in=w_in, w_merge_gate=w_merge_gate, w_branch=w_branch,
                   w_out=w_out, w_ffn_up=w_ffn_up, w_ffn_down=w_ffn_down, a_mu=a_mu, a_w0=a_w0, a_w_up=a_w_up,
                   a_a0=a_a0, a_a_up=a_a_up, a_g_up=a_g_up, a_k_k=a_k_k, a_k_a=a_k_a,
                   a_r_k=a_r_k.reshape(a_r_k.shape[0], -1), a_ln_w=a_ln_w, a_ln_b=a_ln_b, b_alpha_up=b_alpha_up,
                   b_alpha_bias=b_alpha_bias, b_norm=b_norm, c_rel_bias=c_rel_bias, d_conv_w=d_conv_w,
                   d_a_log=d_a_log, d_dt_bias=d_dt_bias, d_norm=d_norm)
    bsz = x_prompt.shape[0]
    depth = w_in.shape[0]
    y_p, y_s = x_prompt, x_sample
    new_p, new_s = [], []
    for l in range(depth):
        lw = _prep_layer(l, weights)
        zero_state = (jnp.zeros((bsz, 1, A_IN), F32), jnp.zeros((bsz, A_HEADS, A_HD, A_HD), F32),
                      jnp.zeros((bsz, B_HEADS, B_DK, B_DV), F32), None, None,
                      jnp.zeros((bsz, D_CONV - 1, 3 * MIX_W), F32), jnp.zeros((bsz, D_HEADS, D_HD, D_HD), F32))
        y_p, st_p = _layer(y_p, zero_state, lw)
        y_s, st_s = _layer(y_s, (state_rwkv_shift[l], state_rwkv[l], state_gla[l], cache_band_k[l],
                                 cache_band_v[l], state_dn_conv[l], state_dn[l]), lw)
        new_p.append(st_p)
        new_s.append(st_s)
    stack = lambda states, i: jnp.stack([s[i] for s in states])
    return ((y_p, y_s) + tuple(stack(new_p, i) for i in range(7)) + tuple(stack(new_s, i) for i in range(7)))
```

```python
import functools

import numpy as np
import jax
import jax.numpy as jnp
from jax import lax
from jax.experimental import pallas as pl
from jax.experimental.pallas import tpu as pltpu

F32 = jnp.float32
BF16 = jnp.bfloat16

D_MODEL = 2048
MIX_W = 512
D_FF = 4 * D_MODEL
CHUNK = 64
LANES = 128
NORM_EPS = 1e-6
VMEM_LIMIT = 56 * 1024 * 1024

A_HEADS, A_HD = 8, 64
A_DECAY_SCALE = 0.6065306597
A_GN_EPS = 64e-5
A_IN = 1792
B_HEADS, B_DK, B_DV = 4, 64, 128
B_GATE_TEMP = 16.0
B_IN = 1552
C_HEADS, C_HD = 8, 64
C_BAND = 8 * CHUNK
C_REL_CLIP = 2 * CHUNK
C_WIN = C_BAND + 2 * CHUNK
C_IN = 1536
D_HEADS, D_HD = 4, 128
D_CONV = 4
D_IN = 2056
IN_TOTAL = A_IN + B_IN + C_IN + D_IN


def _head_pad(cols, head):
    out = []
    for h in range(len(cols) // head):
        out.extend(cols[h * head:(h + 1) * head])
        out.extend([-1] * (LANES - head))
    return out


def _pad_to(cols, width):
    return list(cols) + [-1] * (width - len(cols))


def _build_layout():
    a0, b0, c0, d0 = 0, A_IN, A_IN + B_IN, A_IN + B_IN + C_IN
    rng = lambda s, n: list(range(s, s + n))
    segs = [
        ("a_r", _head_pad(rng(a0, 512), 64)), ("a_k", _head_pad(rng(a0 + 512, 512), 64)),
        ("a_v", _head_pad(rng(a0 + 1024, 512), 64)),
        ("c_q", _head_pad(rng(c0, 512), 64)), ("c_k", _head_pad(rng(c0 + 512, 512), 64)),
        ("c_v", _head_pad(rng(c0 + 1024, 512), 64)),
        ("d_qkv", rng(d0, 1536)), ("d_gate", rng(d0 + 1544, 512)),
        ("b_q", _head_pad(rng(b0, 256), 64)), ("b_k", _head_pad(rng(b0 + 256, 256), 64)),
        ("b_v", rng(b0 + 512, 512)), ("b_rg", rng(b0 + 1040, 512)),
        ("a_wa", rng(a0 + 1536, 128)), ("a_gl", rng(a0 + 1664, 128)),
        ("b_al", _pad_to(rng(b0 + 1024, 16), LANES)), ("d_ba", _pad_to(rng(d0 + 1536, 8), LANES)),
    ]
    idx, off = [], {}
    for name, cols in segs:
        off[name] = len(idx)
        idx.extend(cols)
    return np.asarray(idx, np.int32), off


_P_IDX, _P_OFF = _build_layout()
P_COLS = int(_P_IDX.shape[0])
_A_SLOT_IDX = np.asarray(_head_pad(list(range(512)), 64), np.int32)
_B_SLOT_IDX = np.asarray(_head_pad(list(range(256)), 64), np.int32)
_A_ROW_IDX = np.concatenate([_A_SLOT_IDX, np.where(_A_SLOT_IDX >= 0, _A_SLOT_IDX + 512, -1),
                             np.where(_A_SLOT_IDX >= 0, _A_SLOT_IDX + 1024, -1),
                             np.arange(1536, 1792, dtype=np.int32)]).astype(np.int32)
A_ROW = int(_A_ROW_IDX.shape[0])


def _take_static(x, idx, axis):
    idx = np.asarray(idx)
    axis = axis % x.ndim
    pieces, i = [], 0
    while i < len(idx):
        j = i + 1
        if idx[i] < 0:
            while j < len(idx) and idx[j] < 0:
                j += 1
            pieces.append(jnp.zeros(x.shape[:axis] + (j - i,) + x.shape[axis + 1:], x.dtype))
        else:
            while j < len(idx) and idx[j] == idx[j - 1] + 1:
                j += 1
            pieces.append(lax.slice_in_dim(x, int(idx[i]), int(idx[i]) + (j - i), axis=axis))
        i = j
    return pieces[0] if len(pieces) == 1 else jnp.concatenate(pieces, axis=axis)


def _take_cols(x, idx):
    return _take_static(x, idx, -1)


def _take_rows(x, idx):
    return _take_static(x, idx, -2)


def _a_row_to_orig():
    pos = np.zeros(A_IN, np.int32)
    base = {0: _P_OFF["a_r"], 1: _P_OFF["a_k"], 2: _P_OFF["a_v"]}
    for i in range(1536):
        part, j = divmod(i, 512)
        pos[i] = base[part] + (j // 64) * LANES + (j % 64)
    pos[1536:1664] = _P_OFF["a_wa"] + np.arange(128)
    pos[1664:1792] = _P_OFF["a_gl"] + np.arange(128)
    return pos


_A_ORIG_POS = _a_row_to_orig()


def _mm(a, b):
    return jnp.dot(a.astype(BF16), b.astype(BF16), preferred_element_type=F32)


def _mm_nt(a, b):
    return lax.dot_general(a.astype(BF16), b.astype(BF16), (((1,), (1,)), ((), ())),
                           preferred_element_type=F32)


def _split(x):
    hi = x.astype(BF16)
    lo = (x - hi.astype(F32)).astype(BF16)
    return hi, lo


def _mm3(a, b):
    ah, al = _split(a)
    bh, bl = _split(b)
    return (jnp.dot(ah, bh, preferred_element_type=F32) + jnp.dot(ah, bl, preferred_element_type=F32)
            + jnp.dot(al, bh, preferred_element_type=F32))


def _mm_exact_lhs(a_bf16, b):
    bh, bl = _split(b)
    return jnp.dot(a_bf16, bh, preferred_element_type=F32) + jnp.dot(a_bf16, bl, preferred_element_type=F32)


def _iota2(shape, axis):
    return lax.broadcasted_iota(jnp.int32, shape, axis)


def _tri_inverse(a):
    n = a.shape[0]
    eye = (_iota2((n, n), 0) == _iota2((n, n), 1)).astype(F32)
    t = eye + a
    x = a
    for _ in range(int(np.log2(n)) - 1):
        x = _mm3(x, x)
        t = t + _mm3(t, x)
    return t


def _tri_inverse_many(mats):
    n = mats[0].shape[0]
    eye = (_iota2((n, n), 0) == _iota2((n, n), 1)).astype(F32)
    ts = [eye + a for a in mats]
    xs = list(mats)
    for _ in range(int(np.log2(n)) - 1):
        xs = [_mm(x, x) for x in xs]
        ts = [t + _mm(t, x) for t, x in zip(ts, xs)]
    return ts


def _sigmoid(x):
    return 1.0 / (1.0 + jnp.exp(-x))


def _softplus(x):
    return jnp.maximum(x, 0.0) + jnp.log(1.0 + jnp.exp(-jnp.abs(x)))


def _rms(x, gain):
    return x * lax.rsqrt(jnp.mean(x * x, axis=-1, keepdims=True) + NORM_EPS) * gain


def _lower_tri_bf16(n):
    return (_iota2((n, n), 0) >= _iota2((n, n), 1)).astype(BF16)


def _shift_rows(x, s, carry_rows):
    rows = _iota2(x.shape, 0)
    out = pltpu.roll(x, s, 0)
    for t in range(s):
        out = jnp.where(rows == t, jnp.broadcast_to(carry_rows[t], x.shape), out)
    return out


def _inproj_kernel(x_ref, g_ref, w_ref, u_ref, p_ref, u_scr):
    @pl.when(pl.program_id(1) == 0)
    def _():
        u = _rms(x_ref[...], g_ref[...]).astype(BF16)
        u_scr[...] = u
        u_ref[...] = u

    p_ref[...] = jnp.dot(u_scr[...], w_ref[...], preferred_element_type=F32)


def _inproj(x2, gain, w):
    n = x2.shape[0]
    tm = min(n, 1024)
    tn = 512
    return pl.pallas_call(
        _inproj_kernel,
        grid=(n // tm, P_COLS // tn),
        in_specs=[pl.BlockSpec((tm, D_MODEL), lambda i, j: (i, 0)),
                  pl.BlockSpec((1, D_MODEL), lambda i, j: (0, 0)),
                  pl.BlockSpec((D_MODEL, tn), lambda i, j: (0, j))],
        out_specs=[pl.BlockSpec((tm, D_MODEL), lambda i, j: (i, 0)),
                   pl.BlockSpec((tm, tn), lambda i, j: (i, j))],
        out_shape=[jax.ShapeDtypeStruct((n, D_MODEL), BF16), jax.ShapeDtypeStruct((n, P_COLS), F32)],
        scratch_shapes=[pltpu.VMEM((tm, D_MODEL), BF16)],
        compiler_params=pltpu.CompilerParams(dimension_semantics=("parallel", "arbitrary"),
                                             vmem_limit_bytes=VMEM_LIMIT),
        name="inproj",
    )(x2, gain, w)


def _rwkv_kernel(pr_ref, pk_ref, pv_ref, pwa_ref, pgl_ref, sh0_ref, s0_ref, mu_ref, w0_ref, wup_ref,
                 a0_ref, aup_ref, gup_ref, kk_ref, ka_ref, rk_ref, lnw_ref, lnb_ref,
                 y_ref, s_scr, carry_scr):
    c = pl.program_id(1)
    C = CHUNK
    HW = A_HEADS * LANES

    @pl.when(c == 0)
    def _():
        carry_scr[...] = sh0_ref[...]
        s_scr[...] = s0_ref[...]

    def token_shift(p_ref, lo, hi):
        p = p_ref[...]
        prev = _shift_rows(p, 1, [carry_scr[:, lo:hi]])
        carry_scr[:, lo:hi] = p[C - 1:C, :]
        return p + (prev - p) * mu_ref[:, lo:hi]

    xr = token_shift(pr_ref, 0, HW)
    xk = token_shift(pk_ref, HW, 2 * HW)
    xv = token_shift(pv_ref, 2 * HW, 3 * HW)
    xwa = token_shift(pwa_ref, 3 * HW, 3 * HW + LANES)
    xgl = token_shift(pgl_ref, 3 * HW + LANES, 3 * HW + 2 * LANES)

    logw = -A_DECAY_SCALE * _sigmoid(w0_ref[...] + _mm(jnp.tanh(xwa), wup_ref[...]))
    a = _sigmoid(a0_ref[...] + _mm(xwa, aup_ref[...]))
    g = _mm(_sigmoid(xgl), gup_ref[...])
    kkv = xk * kk_ref[...]
    kmod = xk * (1.0 + (a - 1.0) * ka_ref[...])
    cum = _mm_exact_lhs(_lower_tri_bf16(C), logw)

    ri, ci = _iota2((2 * C, C), 0), _iota2((2 * C, C), 1)
    mask2 = jnp.logical_or(ri % C > ci, jnp.logical_and(ri >= C, ri - C == ci))
    real = _iota2((C, LANES), 1) < A_HD
    heads = range(A_HEADS)
    sls = [slice(h * LANES, (h + 1) * LANES) for h in heads]

    lhs2, bhs, khs, bks, vs = [], [], [], [], []
    for sl in sls:
        kk_h = kkv[:, sl]
        kk_h = kk_h * lax.rsqrt(jnp.sum(kk_h * kk_h, axis=-1, keepdims=True) + 1e-6)
        cum_h = cum[:, sl]
        g_incl = jnp.exp(cum_h)
        g_inv = jnp.exp(-cum_h)
        g_last = g_incl[C - 1:C, :]
        at = kk_h * jnp.exp(cum_h - logw[:, sl])
        rt = xr[:, sl] * g_incl
        bh = -(kk_h * a[:, sl]) * g_inv
        kh = kmod[:, sl] * g_inv
        lhs2.append(jnp.concatenate([at, rt], axis=0).astype(BF16))
        bhs.append(bh.astype(BF16))
        khs.append(kh.astype(BF16))
        bks.append(jnp.concatenate([bh * g_last, kh * g_last], axis=0).astype(BF16))
        vs.append(xv[:, sl])
    s_old = [s_scr[h] for h in heads]
    aab = [jnp.where(mask2, _mm_nt(lhs2[h], bhs[h]), 0.0) for h in heads]
    aak = [jnp.where(mask2, _mm_nt(lhs2[h], khs[h]), 0.0) for h in heads]
    p0 = [_mm_nt(lhs2[h], s_old[h]) for h in heads]
    t_inv = _tri_inverse_many([aab[h][0:C] for h in heads])
    rhs_u = [p0[h][0:C] + _mm(aak[h][0:C], vs[h]) for h in heads]
    us = [_mm(t_inv[h], rhs_u[h]) for h in heads]
    ys = [p0[h][C:2 * C] + _mm(aab[h][C:2 * C], us[h]) + _mm(aak[h][C:2 * C], vs[h]) for h in heads]
    for h in heads:
        z = jnp.concatenate([us[h], vs[h]], axis=0)
        g_last = jnp.exp(cum[C - 1:C, sls[h]])
        s_scr[h] = s_old[h] * g_last + _mm(z.T, bks[h])
    for h, sl in zip(heads, sls):
        y = ys[h]
        mean = jnp.sum(y, axis=-1, keepdims=True) * (1.0 / A_HD)
        d = jnp.where(real, y - mean, 0.0)
        var = jnp.sum(d * d, axis=-1, keepdims=True) * (1.0 / A_HD)
        yn = d * lax.rsqrt(var + A_GN_EPS) * lnw_ref[:, sl] + lnb_ref[:, sl]
        bonus = jnp.sum(xr[:, sl] * kmod[:, sl] * rk_ref[:, sl], axis=-1, keepdims=True) * vs[h]
        y_ref[:, sl] = ((yn + bonus) * g[:, sl]).astype(BF16)


def _rwkv(p3, shift_prev, s0, lw):
    bsz, t, _ = p3.shape
    hw = A_HEADS * LANES
    blk = lambda w, j: pl.BlockSpec((None, CHUNK, w), lambda b, c: (b, c, j))
    full = lambda shp: pl.BlockSpec(shp, lambda b, c: (0,) * len(shp))
    return pl.pallas_call(
        _rwkv_kernel,
        grid=(bsz, t // CHUNK),
        in_specs=[blk(hw, _P_OFF["a_r"] // hw), blk(hw, _P_OFF["a_k"] // hw), blk(hw, _P_OFF["a_v"] // hw),
                  blk(LANES, _P_OFF["a_wa"] // LANES), blk(LANES, _P_OFF["a_gl"] // LANES),
                  pl.BlockSpec((None, 1, A_ROW), lambda b, c: (b, 0, 0)),
                  pl.BlockSpec((None, A_HEADS, LANES, LANES), lambda b, c: (b, 0, 0, 0)),
                  full((1, A_ROW)), full((1, hw)), full((LANES, hw)), full((1, hw)), full((LANES, hw)),
                  full((LANES, hw)), full((1, hw)), full((1, hw)), full((1, hw)), full((1, hw)), full((1, hw))],
        out_specs=[pl.BlockSpec((None, CHUNK, hw), lambda b, c: (b, c, 0)),
                   pl.BlockSpec((None, A_HEADS, LANES, LANES), lambda b, c: (b, 0, 0, 0))],
        out_shape=[jax.ShapeDtypeStruct((bsz, t, hw), BF16),
                   jax.ShapeDtypeStruct((bsz, A_HEADS, LANES, LANES), F32)],
        scratch_shapes=[pltpu.VMEM((1, A_ROW), F32)],
        compiler_params=pltpu.CompilerParams(dimension_semantics=("parallel", "arbitrary"),
                                             vmem_limit_bytes=VMEM_LIMIT),
        name="rwkv7",
    )(p3, p3, p3, p3, p3, shift_prev, s0, lw["a_mu"], lw["a_w0"], lw["a_w_up"], lw["a_a0"], lw["a_a_up"],
      lw["a_g_up"], lw["a_k_k"], lw["a_k_a"], lw["a_r_k"], lw["a_ln_w"], lw["a_ln_b"])


def _gla_kernel(q_ref, k_ref, v_ref, rg_ref, al_ref, s0_ref, aup_ref, ab_ref, nw_ref,
                y_ref, s_scr):
    c = pl.program_id(1)
    C = CHUNK

    @pl.when(c == 0)
    def _():
        s_scr[...] = s0_ref[...]

    x = _mm(al_ref[...], aup_ref[...]) + ab_ref[...]
    loga = -_softplus(-x) * (1.0 / B_GATE_TEMP)
    cum = _mm_exact_lhs(_lower_tri_bf16(C), loga)
    causal = _iota2((C, C), 0) >= _iota2((C, C), 1)
    q, k, v, rg = q_ref[...], k_ref[...], v_ref[...], rg_ref[...]

    heads = range(B_HEADS)
    sls = [slice(h * LANES, (h + 1) * LANES) for h in heads]
    qts, kts, kds, vhs, elast = [], [], [], [], []
    for sl in sls:
        b = cum[:, sl]
        b_last = b[C - 1:C, :]
        qts.append((q[:, sl] * (B_DK ** -0.5) * jnp.exp(b)).astype(BF16))
        kts.append((k[:, sl] * jnp.exp(-b)).astype(BF16))
        kds.append((k[:, sl] * jnp.exp(b_last - b)).astype(BF16))
        vhs.append(v[:, sl].astype(BF16))
        elast.append(jnp.exp(b_last))
    s_old = [s_scr[h] for h in heads]
    atts = [jnp.where(causal, _mm_nt(qts[h], kts[h]), 0.0) for h in heads]
    inter = [_mm_nt(qts[h], s_old[h]) for h in heads]
    outs = [inter[h] + _mm(atts[h], vhs[h]) for h in heads]
    zeros = jnp.zeros((C, LANES), BF16)
    for h in heads:
        vt = jnp.concatenate([v[:, sls[h]], jnp.zeros((C, LANES), F32)], axis=0).T
        s_scr[h] = s_old[h] * elast[h] + _mm(vt, jnp.concatenate([kds[h], zeros], axis=0))
    for h, sl in zip(heads, sls):
        o = outs[h]
        o = o * lax.rsqrt(jnp.mean(o * o, axis=-1, keepdims=True) + NORM_EPS)
        rg_h = rg[:, sl]
        y_ref[:, sl] = (o * nw_ref[:, sl] * (rg_h * _sigmoid(rg_h))).astype(BF16)


def _gla(p3, s0, lw):
    bsz, t, _ = p3.shape
    blk = lambda w, j: pl.BlockSpec((None, CHUNK, w), lambda b, c: (b, c, j))
    full = lambda shp: pl.BlockSpec(shp, lambda b, c: (0,) * len(shp))
    st = pl.BlockSpec((None, B_HEADS, LANES, LANES), lambda b, c: (b, 0, 0, 0))
    return pl.pallas_call(
        _gla_kernel,
        grid=(bsz, t // CHUNK),
        in_specs=[blk(MIX_W, _P_OFF["b_q"] // MIX_W), blk(MIX_W, _P_OFF["b_k"] // MIX_W),
                  blk(MIX_W, _P_OFF["b_v"] // MIX_W), blk(MIX_W, _P_OFF["b_rg"] // MIX_W),
                  blk(LANES, _P_OFF["b_al"] // LANES), st,
                  full((LANES, MIX_W)), full((1, MIX_W)), full((1, MIX_W))],
        out_specs=[pl.BlockSpec((None, CHUNK, MIX_W), lambda b, c: (b, c, 0)), st],
        out_shape=[jax.ShapeDtypeStruct((bsz, t, MIX_W), BF16),
                   jax.ShapeDtypeStruct((bsz, B_HEADS, LANES, LANES), F32)],
        compiler_params=pltpu.CompilerParams(dimension_semantics=("parallel", "arbitrary"),
                                             vmem_limit_bytes=VMEM_LIMIT),
        name="gla",
    )(p3, p3, p3, p3, p3, s0, lw["b_alpha_up"], lw["b_alpha_bias"], lw["b_norm"])


def _band_kernel(q_ref, k_ref, v_ref, g_ref, y_ref, bias_scr, *, pad_rows):
    b = pl.program_id(0)
    c = pl.program_id(1)
    C = CHUNK

    @pl.when(jnp.logical_and(b == 0, c == 0))
    def _():
        rows = _iota2((C, C_WIN), 0)
        for h in range(C_HEADS):
            x = jnp.broadcast_to(g_ref[h:h + 1, :], (C, C_WIN))
            for bit in range(6):
                x = jnp.where(((rows >> bit) & 1) == 1, pltpu.roll(x, 1 << bit, 1), x)
            bias_scr[h] = x

    start = pl.multiple_of(c * C, C)
    col = _iota2((C, C_WIN), 1)
    valid = jnp.logical_and(col < C_BAND + C, col + c * C >= pad_rows)
    heads = range(C_HEADS)
    sls = [slice(h * LANES, (h + 1) * LANES) for h in heads]
    scores = [_mm_nt((q_ref[:, sl] * (C_HD ** -0.5)).astype(BF16), k_ref[pl.ds(start, C_WIN), sl]) for sl in sls]
    probs, denoms = [], []
    for h in heads:
        s = jnp.where(valid, scores[h] + bias_scr[h], -jnp.inf)
        p = jnp.exp(s - jnp.max(s, axis=-1, keepdims=True))
        denoms.append(jnp.sum(p, axis=-1, keepdims=True))
        probs.append(p.astype(BF16))
    outs = [_mm(probs[h], v_ref[pl.ds(start, C_WIN), sls[h]]) for h in heads]
    for h, sl in zip(heads, sls):
        y_ref[:, sl] = (outs[h] / denoms[h]).astype(BF16)


def _band(p3, kpad, vpad, gtab, pad_rows):
    bsz, t, _ = p3.shape
    hw = C_HEADS * LANES
    tp = kpad.shape[1]
    kv = pl.BlockSpec((None, tp, hw), lambda b, c: (b, 0, 0))
    return pl.pallas_call(
        functools.partial(_band_kernel, pad_rows=pad_rows),
        grid=(bsz, t // CHUNK),
        in_specs=[pl.BlockSpec((None, CHUNK, hw), lambda b, c: (b, c, _P_OFF["c_q"] // hw)), kv, kv,
                  pl.BlockSpec((C_HEADS, C_WIN), lambda b, c: (0, 0))],
        out_specs=pl.BlockSpec((None, CHUNK, hw), lambda b, c: (b, c, 0)),
        out_shape=jax.ShapeDtypeStruct((bsz, t, hw), BF16),
        scratch_shapes=[pltpu.VMEM((C_HEADS, CHUNK, C_WIN), F32)],
        compiler_params=pltpu.CompilerParams(dimension_semantics=("arbitrary", "arbitrary"),
                                             vmem_limit_bytes=VMEM_LIMIT),
        name="band_attn",
    )(p3, kpad, vpad, gtab)


def _dn_kernel(qkv_ref, ba_ref, gate_ref, cb0_ref, s0_ref, cw_ref, alog_ref, dtb_ref, nw_ref,
               y_ref, s_scr, carry_scr):
    c = pl.program_id(1)
    C = CHUNK

    @pl.when(c == 0)
    def _():
        carry_scr[...] = cb0_ref[...]
        s_scr[...] = s0_ref[...]

    x = qkv_ref[...]
    c0, c1, c2 = carry_scr[5:6, :], carry_scr[6:7, :], carry_scr[7:8, :]
    conv = (_shift_rows(x, 3, [c0, c1, c2]) * cw_ref[0:1, :] + _shift_rows(x, 2, [c1, c2]) * cw_ref[1:2, :]
            + _shift_rows(x, 1, [c2]) * cw_ref[2:3, :] + x * cw_ref[3:4, :])
    carry_scr[...] = x[C - 8:C, :]
    qkv = conv * _sigmoid(conv)

    ba = ba_ref[...]
    beta_all = _sigmoid(ba)
    g_all = -jnp.exp(alog_ref[...]) * _softplus(ba + dtb_ref[...])
    tri = _lower_tri_bf16(C)
    cum_all = _mm_exact_lhs(tri, g_all)
    lane = _iota2((C, LANES), 1)
    col_of = lambda arr, j: jnp.sum(jnp.where(lane == j, arr, 0.0), axis=-1, keepdims=True)
    ri, ci = _iota2((C, C), 0), _iota2((C, C), 1)
    after = (ri > ci).astype(F32)
    gate = gate_ref[...]

    heads = range(D_HEADS)
    sls = [slice(h * LANES, (h + 1) * LANES) for h in heads]
    betas = [col_of(beta_all, h) for h in heads]
    cums = [col_of(cum_all, D_HEADS + h) for h in heads]
    diffs = [_mm_exact_lhs(tri, col_of(g_all, D_HEADS + h) * after) for h in heads]
    decays = [jnp.where(ri >= ci, jnp.exp(diffs[h]), 0.0) for h in heads]
    qs, ks, vs = [], [], []
    for h in heads:
        q_h = qkv[:, h * LANES:(h + 1) * LANES]
        k_h = qkv[:, MIX_W + h * LANES:MIX_W + (h + 1) * LANES]
        qs.append(q_h * lax.rsqrt(jnp.sum(q_h * q_h, axis=-1, keepdims=True) + 1e-6) * (D_HD ** -0.5))
        ks.append(k_h * lax.rsqrt(jnp.sum(k_h * k_h, axis=-1, keepdims=True) + 1e-6))
        vs.append(qkv[:, 2 * MIX_W + h * LANES:2 * MIX_W + (h + 1) * LANES])
    kbf = [k.astype(BF16) for k in ks]
    kks = [_mm_nt(kbf[h], kbf[h]) for h in heads]
    atts = [_mm_nt(qs[h], kbf[h]) * decays[h] for h in heads]
    t_inv = _tri_inverse_many([jnp.where(ri > ci, -(betas[h] * decays[h] * kks[h]), 0.0) for h in heads])
    ecums = [jnp.exp(cums[h]) for h in heads]
    uws = [_mm(t_inv[h], jnp.concatenate([vs[h] * betas[h], ks[h] * (betas[h] * ecums[h])], axis=1)) for h in heads]
    s_old = [s_scr[h] for h in heads]
    deltas = [uws[h][:, 0:LANES] - _mm(uws[h][:, LANES:2 * LANES], s_old[h]) for h in heads]
    outs = [_mm(qs[h] * ecums[h], s_old[h]) + _mm(atts[h], deltas[h]) for h in heads]
    zeros = jnp.zeros((C, LANES), F32)
    for h in heads:
        cum_last = cums[h][C - 1:C, :]
        kdt = jnp.concatenate([ks[h] * jnp.exp(cum_last - cums[h]), zeros], axis=0).T
        s_scr[h] = s_old[h] * jnp.exp(cum_last) + _mm(kdt, jnp.concatenate([deltas[h], zeros], axis=0))
    for h, sl in zip(heads, sls):
        o = outs[h]
        o = o * lax.rsqrt(jnp.mean(o * o, axis=-1, keepdims=True) + NORM_EPS) * nw_ref[...]
        gt = gate[:, sl]
        y_ref[:, sl] = (o * (gt * _sigmoid(gt))).astype(BF16)


def _dn(p3, conv_buf, s0, lw):
    bsz, t, _ = p3.shape
    w3 = 3 * MIX_W
    full = lambda shp: pl.BlockSpec(shp, lambda b, c: (0,) * len(shp))
    st = pl.BlockSpec((None, D_HEADS, LANES, LANES), lambda b, c: (b, 0, 0, 0))
    return pl.pallas_call(
        _dn_kernel,
        grid=(bsz, t // CHUNK),
        in_specs=[pl.BlockSpec((None, CHUNK, w3), lambda b, c: (b, c, _P_OFF["d_qkv"] // w3)),
                  pl.BlockSpec((None, CHUNK, LANES), lambda b, c: (b, c, _P_OFF["d_ba"] // LANES)),
                  pl.BlockSpec((None, CHUNK, MIX_W), lambda b, c: (b, c, _P_OFF["d_gate"] // MIX_W)),
                  pl.BlockSpec((None, 8, w3), lambda b, c: (b, 0, 0)), st,
                  full((D_CONV, w3)), full((1, LANES)), full((1, LANES)), full((1, LANES))],
        out_specs=[pl.BlockSpec((None, CHUNK, MIX_W), lambda b, c: (b, c, 0)), st],
        out_shape=[jax.ShapeDtypeStruct((bsz, t, MIX_W), BF16),
                   jax.ShapeDtypeStruct((bsz, D_HEADS, LANES, LANES), F32)],
        scratch_shapes=[pltpu.VMEM((8, w3), F32)],
        compiler_params=pltpu.CompilerParams(dimension_semantics=("parallel", "arbitrary"),
                                             vmem_limit_bytes=VMEM_LIMIT),
        name="deltanet",
    )(p3, p3, p3, conv_buf, s0, lw["d_conv_w"], lw["d_a_log"], lw["d_dt_bias"], lw["d_norm"])


def _merge_kernel(u_ref, ya_ref, yb_ref, yc_ref, yd_ref, x_ref, wg_ref, wba_ref, wbb_ref, wbc_ref, wbd_ref,
                  wo_ref, gain_ref, h_ref, acc_ref):
    j = pl.program_id(1)

    @pl.when(j == 0)
    def _():
        acc_ref[...] = jnp.zeros_like(acc_ref)

    u = u_ref[...]
    merged = None
    for b, (y_r, wb_r) in enumerate(((ya_ref, wba_ref), (yb_ref, wbb_ref), (yc_ref, wbc_ref), (yd_ref, wbd_ref))):
        gate = _sigmoid(jnp.dot(u, wg_ref[b], preferred_element_type=F32))
        term = gate * jnp.dot(y_r[...], wb_r[...], preferred_element_type=F32)
        merged = term if merged is None else merged + term
    acc_ref[...] += jnp.dot(merged.astype(BF16), wo_ref[...], preferred_element_type=F32)

    @pl.when(j == pl.num_programs(1) - 1)
    def _():
        h_ref[...] = x_ref[...] + _rms(acc_ref[...], gain_ref[...])


def _merge(u, ya, yb, yc, yd, x2, lw):
    n = x2.shape[0]
    tm = min(n, 512)
    tn = 256
    row = lambda w: pl.BlockSpec((tm, w), lambda i, j: (i, 0))
    wcol = lambda k: pl.BlockSpec((k, tn), lambda i, j: (0, j))
    return pl.pallas_call(
        _merge_kernel,
        grid=(n // tm, D_MODEL // tn),
        in_specs=[row(D_MODEL), row(ya.shape[1]), row(yb.shape[1]), row(yc.shape[1]), row(yd.shape[1]),
                  row(D_MODEL),
                  pl.BlockSpec((4, D_MODEL, tn), lambda i, j: (0, 0, j)),
                  wcol(ya.shape[1]), wcol(yb.shape[1]), wcol(yc.shape[1]), wcol(yd.shape[1]),
                  pl.BlockSpec((tn, D_MODEL), lambda i, j: (j, 0)),
                  pl.BlockSpec((1, D_MODEL), lambda i, j: (0, 0))],
        out_specs=row(D_MODEL),
        out_shape=jax.ShapeDtypeStruct((n, D_MODEL), F32),
        scratch_shapes=[pltpu.VMEM((tm, D_MODEL), F32)],
        compiler_params=pltpu.CompilerParams(dimension_semantics=("parallel", "arbitrary"),
                                             vmem_limit_bytes=VMEM_LIMIT),
        name="merge",
    )(u, ya, yb, yc, yd, x2, lw["w_merge_gate"], lw["wb_a"], lw["wb_b"], lw["wb_c"], lw["wb_d"],
      lw["w_out"], lw["norm_mix_post"])


def _ffn_kernel(h_ref, g1_ref, wu_ref, wd_ref, g2_ref, o_ref, z_scr, acc_ref):
    j = pl.program_id(1)

    @pl.when(j == 0)
    def _():
        z_scr[...] = _rms(h_ref[...], g1_ref[...]).astype(BF16)
        acc_ref[...] = jnp.zeros_like(acc_ref)

    a = jnp.maximum(jnp.dot(z_scr[...], wu_ref[...], preferred_element_type=F32), 0.0)
    acc_ref[...] += jnp.dot((a * a).astype(BF16), wd_ref[...], preferred_element_type=F32)

    @pl.when(j == pl.num_programs(1) - 1)
    def _():
        o_ref[...] = h_ref[...] + _rms(acc_ref[...], g2_ref[...])


def _ffn(h, lw):
    n = h.shape[0]
    tm = min(n, 512)
    tf = 1024
    vec = pl.BlockSpec((1, D_MODEL), lambda i, j: (0, 0))
    return pl.pallas_call(
        _ffn_kernel,
        grid=(n // tm, D_FF // tf),
        in_specs=[pl.BlockSpec((tm, D_MODEL), lambda i, j: (i, 0)), vec,
                  pl.BlockSpec((D_MODEL, tf), lambda i, j: (0, j)),
                  pl.BlockSpec((tf, D_MODEL), lambda i, j: (j, 0)), vec],
        out_specs=pl.BlockSpec((tm, D_MODEL), lambda i, j: (i, 0)),
        out_shape=jax.ShapeDtypeStruct((n, D_MODEL), F32),
        scratch_shapes=[pltpu.VMEM((tm, D_MODEL), BF16), pltpu.VMEM((tm, D_MODEL), F32)],
        compiler_params=pltpu.CompilerParams(dimension_semantics=("parallel", "arbitrary"),
                                             vmem_limit_bytes=VMEM_LIMIT),
        name="ffn",
    )(h, lw["norm_ffn_pre"], lw["w_ffn_up"], lw["w_ffn_down"], lw["norm_ffn_post"])


def _band_bias_row(rel):
    m = np.arange(C_WIN)
    idx = np.where(m <= C_BAND + CHUNK, np.clip(C_BAND - m, -C_REL_CLIP, C_REL_CLIP) + C_REL_CLIP, 2 * C_REL_CLIP)
    return rel[:, idx.astype(np.int32)]


def _prep_layer(l, w):
    row = lambda v: v.reshape(1, -1)
    a_slot = lambda v: _take_cols(v, _A_SLOT_IDX)
    zeros64 = jnp.zeros((64, A_HEADS * LANES), F32)
    d_lane = np.full(LANES, -1, np.int32)
    d_lane[D_HEADS:2 * D_HEADS] = np.arange(D_HEADS)
    wb = w["w_branch"][l]
    return {
        "norm_mix_pre": row(w["norm_mix_pre"][l]), "norm_mix_post": row(w["norm_mix_post"][l]),
        "norm_ffn_pre": row(w["norm_ffn_pre"][l]), "norm_ffn_post": row(w["norm_ffn_post"][l]),
        "w_in": _take_cols(w["w_in"][l], _P_IDX).astype(BF16),
        "w_merge_gate": w["w_merge_gate"][l].astype(BF16),
        "wb_a": _take_rows(wb[0], _A_SLOT_IDX).astype(BF16), "wb_b": wb[1].astype(BF16),
        "wb_c": _take_rows(wb[2], _A_SLOT_IDX).astype(BF16), "wb_d": wb[3].astype(BF16),
        "w_out": w["w_out"][l].astype(BF16),
        "w_ffn_up": w["w_ffn_up"][l].astype(BF16), "w_ffn_down": w["w_ffn_down"][l].astype(BF16),
        "a_mu": _take_cols(row(w["a_mu"][l]), _A_ROW_IDX),
        "a_w0": a_slot(row(w["a_w0"][l])), "a_a0": a_slot(row(w["a_a0"][l])),
        "a_w_up": jnp.concatenate([a_slot(w["a_w_up"][l]), zeros64], axis=0).astype(BF16),
        "a_a_up": jnp.concatenate([zeros64, a_slot(w["a_a_up"][l])], axis=0).astype(BF16),
        "a_g_up": a_slot(w["a_g_up"][l]).astype(BF16),
        "a_k_k": a_slot(row(w["a_k_k"][l])), "a_k_a": a_slot(row(w["a_k_a"][l])),
        "a_r_k": a_slot(row(w["a_r_k"][l])),
        "a_ln_w": a_slot(row(w["a_ln_w"][l])), "a_ln_b": a_slot(row(w["a_ln_b"][l])),
        "b_alpha_up": jnp.pad(_take_cols(w["b_alpha_up"][l], _B_SLOT_IDX), ((0, LANES - 16), (0, 0))).astype(BF16),
        "b_alpha_bias": _take_cols(row(w["b_alpha_bias"][l]), _B_SLOT_IDX),
        "b_norm": row(w["b_norm"][l]),
        "c_gtab": _band_bias_row(w["c_rel_bias"][l]),
        "d_conv_w": w["d_conv_w"][l],
        "d_a_log": _take_cols(row(w["d_a_log"][l]), d_lane), "d_dt_bias": _take_cols(row(w["d_dt_bias"][l]), d_lane),
        "d_norm": row(w["d_norm"][l]),
    }


def _layer(x, st, lw):
    bsz, t, _ = x.shape
    n = bsz * t
    shift_prev, s_a, s_b, band_k, band_v, conv_buf, s_d = st
    x2 = x.reshape(n, D_MODEL)
    u, p = _inproj(x2, lw["norm_mix_pre"], lw["w_in"])
    p3 = p.reshape(bsz, t, P_COLS)

    def seg(name, width):
        return p3[:, :, _P_OFF[name]:_P_OFF[name] + width]

    y_a, s_a_new = _rwkv(p3, _take_cols(shift_prev, _A_ROW_IDX),
                         jnp.pad(s_a, ((0, 0), (0, 0), (0, LANES - A_HD), (0, LANES - A_HD))), lw)
    s_b_t = jnp.pad(jnp.swapaxes(s_b, -1, -2), ((0, 0), (0, 0), (0, 0), (0, LANES - B_DK)))
    y_b, s_b_new = _gla(p3, s_b_t, lw)
    hw = C_HEADS * LANES
    k_new, v_new = seg("c_k", hw), seg("c_v", hw)
    if band_k is None:
        pad_rows = C_BAND
        front = jnp.zeros((bsz, C_BAND, hw), BF16)
    else:
        pad_rows = 0
        slot = lambda z: jnp.pad(z, ((0, 0), (0, 0), (0, 0), (0, LANES - C_HD))).reshape(bsz, C_BAND, hw).astype(BF16)
        front = None
    tail = jnp.zeros((bsz, CHUNK, hw), BF16)
    kpad = jnp.concatenate([front if band_k is None else slot(band_k), k_new.astype(BF16), tail], axis=1)
    vpad = jnp.concatenate([front if band_v is None else slot(band_v), v_new.astype(BF16), tail], axis=1)
    y_c = _band(p3, kpad, vpad, lw["c_gtab"], pad_rows)
    y_d, s_d_new = _dn(p3, jnp.pad(conv_buf, ((0, 0), (8 - (D_CONV - 1), 0), (0, 0))), s_d, lw)

    h = _merge(u, y_a.reshape(n, -1), y_b.reshape(n, -1), y_c.reshape(n, -1), y_d.reshape(n, -1), x2, lw)
    out = _ffn(h, lw).reshape(bsz, t, D_MODEL)

    keep = min(C_BAND, t)
    unslot = lambda z: z.reshape(bsz, t, C_HEADS, LANES)[:, t - keep:, :, :C_HD]
    new_state = (
        _take_cols(p3[:, t - 1:, :], _A_ORIG_POS),
        s_a_new[:, :, :A_HD, :A_HD],
        jnp.swapaxes(s_b_new[:, :, :, :B_DK], -1, -2),
        unslot(k_new), unslot(v_new),
        seg("d_qkv", 3 * MIX_W)[:, t - (D_CONV - 1):, :],
        s_d_new,
    )
    return out, new_state


def kernel(x_prompt, x_sample, state_rwkv_shift, state_rwkv, state_gla, cache_band_k, cache_band_v, state_dn_conv, state_dn, norm_mix_pre, norm_mix_post, norm_ffn_pre, norm_ffn_post, w_in, w_merge_gate, w_branch, w_out, w_ffn_up, w_ffn_down, a_mu, a_w0, a_w_up, a_a0, a_a_up, a_g_up, a_k_k, a_k_a, a_r_k, a_ln_w, a_ln_b, b_alpha_up, b_alpha_bias, b_norm, c_rel_bias, d_conv_w, d_a_log, d_dt_bias, d_norm):
    weights = dict(norm_mix_pre=norm_mix_pre, norm_mix_post=norm_mix_post, norm_ffn_pre=norm_ffn_pre,
                   norm_ffn_post=norm_ffn_post, w_in=w_in, w_merge_gate=w_merge_gate, w_branch=w_branch,
                   w_out=w_out, w_ffn_up=w_ffn_up, w_ffn_down=w_ffn_down, a_mu=a_mu, a_w0=a_w0, a_w_up=a_w_up,
                   a_a0=a_a0, a_a_up=a_a_up, a_g_up=a_g_up, a_k_k=a_k_k, a_k_a=a_k_a,
                   a_r_k=a_r_k.reshape(a_r_k.shape[0], -1), a_ln_w=a_ln_w, a_ln_b=a_ln_b, b_alpha_up=b_alpha_up,
                   b_alpha_bias=b_alpha_bias, b_norm=b_norm, c_rel_bias=c_rel_bias, d_conv_w=d_conv_w,
                   d_a_log=d_a_log, d_dt_bias=d_dt_bias, d_norm=d_norm)
    bsz = x_prompt.shape[0]
    depth = w_in.shape[0]
    y_p, y_s = x_prompt, x_sample
    new_p, new_s = [], []
    for l in range(depth):
        lw = _prep_layer(l, weights)
        zero_state = (jnp.zeros((bsz, 1, A_IN), F32), jnp.zeros((bsz, A_HEADS, A_HD, A_HD), F32),
                      jnp.zeros((bsz, B_HEADS, B_DK, B_DV), F32), None, None,
                      jnp.zeros((bsz, D_CONV - 1, 3 * MIX_W), F32), jnp.zeros((bsz, D_HEADS, D_HD, D_HD), F32))
        y_p, st_p = _layer(y_p, zero_state, lw)
        y_s, st_s = _layer(y_s, (state_rwkv_shift[l], state_rwkv[l], state_gla[l], cache_band_k[l],
                                 cache_band_v[l], state_dn_conv[l], state_dn[l]), lw)
        new_p.append(st_p)
        new_s.append(st_s)
    stack = lambda states, i: jnp.stack([s[i] for s in states])
    return ((y_p, y_s) + tuple(stack(new_p, i) for i in range(7)) + tuple(stack(new_s, i) for i in range(7)))
```

```python
import functools

import numpy as np
import jax
import jax.numpy as jnp
from jax import lax
from jax.experimental import pallas as pl
from jax.experimental.pallas import tpu as pltpu

F32 = jnp.float32
BF16 = jnp.bfloat16

D_MODEL = 2048
MIX_W = 512
D_FF = 4 * D_MODEL
CHUNK = 64
LANES = 128
NORM_EPS = 1e-6
VMEM_LIMIT = 56 * 1024 * 1024

A_HEADS, A_HD = 8, 64
A_DECAY_SCALE = 0.6065306597
A_GN_EPS = 64e-5
A_IN = 1792
B_HEADS, B_DK, B_DV = 4, 64, 128
B_GATE_TEMP = 16.0
B_IN = 1552
C_HEADS, C_HD = 8, 64
C_BAND = 8 * CHUNK
C_REL_CLIP = 2 * CHUNK
C_WIN = C_BAND + 2 * CHUNK
C_IN = 1536
D_HEADS, D_HD = 4, 128
D_CONV = 4
D_IN = 2056
IN_TOTAL = A_IN + B_IN + C_IN + D_IN


def _head_pad(cols, head):
    out = []
    for h in range(len(cols) // head):
        out.extend(cols[h * head:(h + 1) * head])
        out.extend([-1] * (LANES - head))
    return out


def _pad_to(cols, width):
    return list(cols) + [-1] * (width - len(cols))


def _build_layout():
    a0, b0, c0, d0 = 0, A_IN, A_IN + B_IN, A_IN + B_IN + C_IN
    rng = lambda s, n: list(range(s, s + n))
    segs = [
        ("a_r", _head_pad(rng(a0, 512), 64)), ("a_k", _head_pad(rng(a0 + 512, 512), 64)),
        ("a_v", _head_pad(rng(a0 + 1024, 512), 64)),
        ("c_q", _head_pad(rng(c0, 512), 64)), ("c_k", _head_pad(rng(c0 + 512, 512), 64)),
        ("c_v", _head_pad(rng(c0 + 1024, 512), 64)),
        ("d_qkv", rng(d0, 1536)), ("d_gate", rng(d0 + 1544, 512)),
        ("b_q", _head_pad(rng(b0, 256), 64)), ("b_k", _head_pad(rng(b0 + 256, 256), 64)),
        ("b_v", rng(b0 + 512, 512)), ("b_rg", rng(b0 + 1040, 512)),
        ("a_wa", rng(a0 + 1536, 128)), ("a_gl", rng(a0 + 1664, 128)),
        ("b_al", _pad_to(rng(b0 + 1024, 16), LANES)), ("d_ba", _pad_to(rng(d0 + 1536, 8), LANES)),
    ]
    idx, off = [], {}
    for name, cols in segs:
        off[name] = len(idx)
        idx.extend(cols)
    return np.asarray(idx, np.int32), off


_P_IDX, _P_OFF = _build_layout()
P_COLS = int(_P_IDX.shape[0])
_A_SLOT_IDX = np.asarray(_head_pad(list(range(512)), 64), np.int32)
_B_SLOT_IDX = np.asarray(_head_pad(list(range(256)), 64), np.int32)
_A_ROW_IDX = np.concatenate([_A_SLOT_IDX, np.where(_A_SLOT_IDX >= 0, _A_SLOT_IDX + 512, -1),
                             np.where(_A_SLOT_IDX >= 0, _A_SLOT_IDX + 1024, -1),
                             np.arange(1536, 1792, dtype=np.int32)]).astype(np.int32)
A_ROW = int(_A_ROW_IDX.shape[0])


def _take_static(x, idx, axis):
    idx = np.asarray(idx)
    axis = axis % x.ndim
    pieces, i = [], 0
    while i < len(idx):
        j = i + 1
        if idx[i] < 0:
            while j < len(idx) and idx[j] < 0:
                j += 1
            pieces.append(jnp.zeros(x.shape[:axis] + (j - i,) + x.shape[axis + 1:], x.dtype))
        else:
            while j < len(idx) and idx[j] == idx[j - 1] + 1:
                j += 1
            pieces.append(lax.slice_in_dim(x, int(idx[i]), int(idx[i]) + (j - i), axis=axis))
        i = j
    return pieces[0] if len(pieces) == 1 else jnp.concatenate(pieces, axis=axis)


def _take_cols(x, idx):
    return _take_static(x, idx, -1)


def _take_rows(x, idx):
    return _take_static(x, idx, -2)


def _a_row_to_orig():
    pos = np.zeros(A_IN, np.int32)
    base = {0: _P_OFF["a_r"], 1: _P_OFF["a_k"], 2: _P_OFF["a_v"]}
    for i in range(1536):
        part, j = divmod(i, 512)
        pos[i] = base[part] + (j // 64) * LANES + (j % 64)
    pos[1536:1664] = _P_OFF["a_wa"] + np.arange(128)
    pos[1664:1792] = _P_OFF["a_gl"] + np.arange(128)
    return pos


_A_ORIG_POS = _a_row_to_orig()


def _mm(a, b):
    return jnp.dot(a.astype(BF16), b.astype(BF16), preferred_element_type=F32)


def _mm_nt(a, b):
    return lax.dot_general(a.astype(BF16), b.astype(BF16), (((1,), (1,)), ((), ())),
                           preferred_element_type=F32)


def _split(x):
    hi = x.astype(BF16)
    lo = (x - hi.astype(F32)).astype(BF16)
    return hi, lo


def _mm3(a, b):
    ah, al = _split(a)
    bh, bl = _split(b)
    return (jnp.dot(ah, bh, preferred_element_type=F32) + jnp.dot(ah, bl, preferred_element_type=F32)
            + jnp.dot(al, bh, preferred_element_type=F32))


def _mm_exact_lhs(a_bf16, b):
    bh, bl = _split(b)
    return jnp.dot(a_bf16, bh, preferred_element_type=F32) + jnp.dot(a_bf16, bl, preferred_element_type=F32)


def _iota2(shape, axis):
    return lax.broadcasted_iota(jnp.int32, shape, axis)


def _tri_inverse(a):
    n = a.shape[0]
    eye = (_iota2((n, n), 0) == _iota2((n, n), 1)).astype(F32)
    t = eye + a
    x = a
    for _ in range(int(np.log2(n)) - 1):
        x = _mm3(x, x)
        t = t + _mm3(t, x)
    return t


def _tri_inverse_many(mats):
    n = mats[0].shape[0]
    eye = (_iota2((n, n), 0) == _iota2((n, n), 1)).astype(F32)
    ts = [eye + a for a in mats]
    xs = list(mats)
    for _ in range(int(np.log2(n)) - 1):
        xs = [_mm(x, x) for x in xs]
        ts = [t + _mm(t, x) for t, x in zip(ts, xs)]
    return ts


def _sigmoid(x):
    return 1.0 / (1.0 + jnp.exp(-x))


def _softplus(x):
    return jnp.maximum(x, 0.0) + jnp.log(1.0 + jnp.exp(-jnp.abs(x)))


def _rms(x, gain):
    return x * lax.rsqrt(jnp.mean(x * x, axis=-1, keepdims=True) + NORM_EPS) * gain


def _lower_tri_bf16(n):
    return (_iota2((n, n), 0) >= _iota2((n, n), 1)).astype(BF16)


def _shift_rows(x, s, carry_rows):
    rows = _iota2(x.shape, 0)
    out = pltpu.roll(x, s, 0)
    for t in range(s):
        out = jnp.where(rows == t, jnp.broadcast_to(carry_rows[t], x.shape), out)
    return out


def _inproj_kernel(x_ref, g_ref, w_ref, u_ref, p_ref, u_scr):
    @pl.when(pl.program_id(1) == 0)
    def _():
        u = _rms(x_ref[...], g_ref[...]).astype(BF16)
        u_scr[...] = u
        u_ref[...] = u

    p_ref[...] = jnp.dot(u_scr[...], w_ref[...], preferred_element_type=F32)


def _inproj(x2, gain, w):
    n = x2.shape[0]
    tm = min(n, 1024)
    tn = 512
    return pl.pallas_call(
        _inproj_kernel,
        grid=(n // tm, P_COLS // tn),
        in_specs=[pl.BlockSpec((tm, D_MODEL), lambda i, j: (i, 0)),
                  pl.BlockSpec((1, D_MODEL), lambda i, j: (0, 0)),
                  pl.BlockSpec((D_MODEL, tn), lambda i, j: (0, j))],
        out_specs=[pl.BlockSpec((tm, D_MODEL), lambda i, j: (i, 0)),
                   pl.BlockSpec((tm, tn), lambda i, j: (i, j))],
        out_shape=[jax.ShapeDtypeStruct((n, D_MODEL), BF16), jax.ShapeDtypeStruct((n, P_COLS), F32)],
        scratch_shapes=[pltpu.VMEM((tm, D_MODEL), BF16)],
        compiler_params=pltpu.CompilerParams(dimension_semantics=("parallel", "arbitrary"),
                                             vmem_limit_bytes=VMEM_LIMIT),
        name="inproj",
    )(x2, gain, w)


def _rwkv_kernel(pr_ref, pk_ref, pv_ref, pwa_ref, pgl_ref, sh0_ref, s0_ref, mu_ref, w0_ref, wup_ref,
                 a0_ref, aup_ref, gup_ref, kk_ref, ka_ref, rk_ref, lnw_ref, lnb_ref,
                 y_ref, s_scr, carry_scr):
    c = pl.program_id(1)
    C = CHUNK
    HW = A_HEADS * LANES

    @pl.when(c == 0)
    def _():
        carry_scr[...] = sh0_ref[...]
        s_scr[...] = s0_ref[...]

    def token_shift(p_ref, lo, hi):
        p = p_ref[...]
        prev = _shift_rows(p, 1, [carry_scr[:, lo:hi]])
        carry_scr[:, lo:hi] = p[C - 1:C, :]
        return p + (prev - p) * mu_ref[:, lo:hi]

    xr = token_shift(pr_ref, 0, HW)
    xk = token_shift(pk_ref, HW, 2 * HW)
    xv = token_shift(pv_ref, 2 * HW, 3 * HW)
    xwa = token_shift(pwa_ref, 3 * HW, 3 * HW + LANES)
    xgl = token_shift(pgl_ref, 3 * HW + LANES, 3 * HW + 2 * LANES)

    logw = -A_DECAY_SCALE * _sigmoid(w0_ref[...] + _mm(jnp.tanh(xwa), wup_ref[...]))
    a = _sigmoid(a0_ref[...] + _mm(xwa, aup_ref[...]))
    g = _mm(_sigmoid(xgl), gup_ref[...])
    kkv = xk * kk_ref[...]
    kmod = xk * (1.0 + (a - 1.0) * ka_ref[...])
    cum = _mm_exact_lhs(_lower_tri_bf16(C), logw)

    ri, ci = _iota2((2 * C, C), 0), _iota2((2 * C, C), 1)
    mask2 = jnp.logical_or(ri % C > ci, jnp.logical_and(ri >= C, ri - C == ci))
    real = _iota2((C, LANES), 1) < A_HD
    heads = range(A_HEADS)
    sls = [slice(h * LANES, (h + 1) * LANES) for h in heads]

    lhs2, bhs, khs, bks, vs = [], [], [], [], []
    for sl in sls:
        kk_h = kkv[:, sl]
        kk_h = kk_h * lax.rsqrt(jnp.sum(kk_h * kk_h, axis=-1, keepdims=True) + 1e-6)
        cum_h = cum[:, sl]
        g_incl = jnp.exp(cum_h)
        g_inv = jnp.exp(-cum_h)
        g_last = g_incl[C - 1:C, :]
        at = kk_h * jnp.exp(cum_h - logw[:, sl])
        rt = xr[:, sl] * g_incl
        bh = -(kk_h * a[:, sl]) * g_inv
        kh = kmod[:, sl] * g_inv
        lhs2.append(jnp.concatenate([at, rt], axis=0).astype(BF16))
        bhs.append(bh.astype(BF16))
        khs.append(kh.astype(BF16))
        bks.append(jnp.concatenate([bh * g_last, kh * g_last], axis=0).astype(BF16))
        vs.append(xv[:, sl])
    s_old = [s_scr[h] for h in heads]
    aab = [jnp.where(mask2, _mm_nt(lhs2[h], bhs[h]), 0.0) for h in heads]
    aak = [jnp.where(mask2, _mm_nt(lhs2[h], khs[h]), 0.0) for h in heads]
    p0 = [_mm_nt(lhs2[h], s_old[h]) for h in heads]
    t_inv = _tri_inverse_many([aab[h][0:C] for h in heads])
    rhs_u = [p0[h][0:C] + _mm(aak[h][0:C], vs[h]) for h in heads]
    us = [_mm(t_inv[h], rhs_u[h]) for h in heads]
    ys = [p0[h][C:2 * C] + _mm(aab[h][C:2 * C], us[h]) + _mm(aak[h][C:2 * C], vs[h]) for h in heads]
    for h in heads:
        z = jnp.concatenate([us[h], vs[h]], axis=0)
        g_last = jnp.exp(cum[C - 1:C, sls[h]])
        s_scr[h] = s_old[h] * g_last + _mm(z.T, bks[h])
    for h, sl in zip(heads, sls):
        y = ys[h]
        mean = jnp.sum(y, axis=-1, keepdims=True) * (1.0 / A_HD)
        d = jnp.where(real, y - mean, 0.0)
        var = jnp.sum(d * d, axis=-1, keepdims=True) * (1.0 / A_HD)
        yn = d * lax.rsqrt(var + A_GN_EPS) * lnw_ref[:, sl] + lnb_ref[:, sl]
        bonus = jnp.sum(xr[:, sl] * kmod[:, sl] * rk_ref[:, sl], axis=-1, keepdims=True) * vs[h]
        y_ref[:, sl] = ((yn + bonus) * g[:, sl]).astype(BF16)


def _rwkv(p3, shift_prev, s0, lw):
    bsz, t, _ = p3.shape
    hw = A_HEADS * LANES
    blk = lambda w, j: pl.BlockSpec((None, CHUNK, w), lambda b, c: (b, c, j))
    full = lambda shp: pl.BlockSpec(shp, lambda b, c: (0,) * len(shp))
    return pl.pallas_call(
        _rwkv_kernel,
        grid=(bsz, t // CHUNK),
        in_specs=[blk(hw, _P_OFF["a_r"] // hw), blk(hw, _P_OFF["a_k"] // hw), blk(hw, _P_OFF["a_v"] // hw),
                  blk(LANES, _P_OFF["a_wa"] // LANES), blk(LANES, _P_OFF["a_gl"] // LANES),
                  pl.BlockSpec((None, 1, A_ROW), lambda b, c: (b, 0, 0)),
                  pl.BlockSpec((None, A_HEADS, LANES, LANES), lambda b, c: (b, 0, 0, 0)),
                  full((1, A_ROW)), full((1, hw)), full((LANES, hw)), full((1, hw)), full((LANES, hw)),
                  full((LANES, hw)), full((1, hw)), full((1, hw)), full((1, hw)), full((1, hw)), full((1, hw))],
        out_specs=[pl.BlockSpec((None, CHUNK, hw), lambda b, c: (b, c, 0)),
                   pl.BlockSpec((None, A_HEADS, LANES, LANES), lambda b, c: (b, 0, 0, 0))],
        out_shape=[jax.ShapeDtypeStruct((bsz, t, hw), BF16),
                   jax.ShapeDtypeStruct((bsz, A_HEADS, LANES, LANES), F32)],
        scratch_shapes=[pltpu.VMEM((1, A_ROW), F32)],
        compiler_params=pltpu.CompilerParams(dimension_semantics=("parallel", "arbitrary"),
                                             vmem_limit_bytes=VMEM_LIMIT),
        name="rwkv7",
    )(p3, p3, p3, p3, p3, shift_prev, s0, lw["a_mu"], lw["a_w0"], lw["a_w_up"], lw["a_a0"], lw["a_a_up"],
      lw["a_g_up"], lw["a_k_k"], lw["a_k_a"], lw["a_r_k"], lw["a_ln_w"], lw["a_ln_b"])


def _gla_kernel(q_ref, k_ref, v_ref, rg_ref, al_ref, s0_ref, aup_ref, ab_ref, nw_ref,
                y_ref, s_scr):
    c = pl.program_id(1)
    C = CHUNK

    @pl.when(c == 0)
    def _():
        s_scr[...] = s0_ref[...]

    x = _mm(al_ref[...], aup_ref[...]) + ab_ref[...]
    loga = -_softplus(-x) * (1.0 / B_GATE_TEMP)
    cum = _mm_exact_lhs(_lower_tri_bf16(C), loga)
    causal = _iota2((C, C), 0) >= _iota2((C, C), 1)
    q, k, v, rg = q_ref[...], k_ref[...], v_ref[...], rg_ref[...]

    heads = range(B_HEADS)
    sls = [slice(h * LANES, (h + 1) * LANES) for h in heads]
    qts, kts, kds, vhs, elast = [], [], [], [], []
    for sl in sls:
        b = cum[:, sl]
        b_last = b[C - 1:C, :]
        qts.append((q[:, sl] * (B_DK ** -0.5) * jnp.exp(b)).astype(BF16))
        kts.append((k[:, sl] * jnp.exp(-b)).astype(BF16))
        kds.append((k[:, sl] * jnp.exp(b_last - b)).astype(BF16))
        vhs.append(v[:, sl].astype(BF16))
        elast.append(jnp.exp(b_last))
    s_old = [s_scr[h] for h in heads]
    atts = [jnp.where(causal, _mm_nt(qts[h], kts[h]), 0.0) for h in heads]
    inter = [_mm_nt(qts[h], s_old[h]) for h in heads]
    outs = [inter[h] + _mm(atts[h], vhs[h]) for h in heads]
    zeros = jnp.zeros((C, LANES), BF16)
    for h in heads:
        vt = jnp.concatenate([v[:, sls[h]], jnp.zeros((C, LANES), F32)], axis=0).T
        s_scr[h] = s_old[h] * elast[h] + _mm(vt, jnp.concatenate([kds[h], zeros], axis=0))
    for h, sl in zip(heads, sls):
        o = outs[h]
        o = o * lax.rsqrt(jnp.mean(o * o, axis=-1, keepdims=True) + NORM_EPS)
        rg_h = rg[:, sl]
        y_ref[:, sl] = (o * nw_ref[:, sl] * (rg_h * _sigmoid(rg_h))).astype(BF16)


def _gla(p3, s0, lw):
    bsz, t, _ = p3.shape
    blk = lambda w, j: pl.BlockSpec((None, CHUNK, w), lambda b, c: (b, c, j))
    full = lambda shp: pl.BlockSpec(shp, lambda b, c: (0,) * len(shp))
    st = pl.BlockSpec((None, B_HEADS, LANES, LANES), lambda b, c: (b, 0, 0, 0))
    return pl.pallas_call(
        _gla_kernel,
        grid=(bsz, t // CHUNK),
        in_specs=[blk(MIX_W, _P_OFF["b_q"] // MIX_W), blk(MIX_W, _P_OFF["b_k"] // MIX_W),
                  blk(MIX_W, _P_OFF["b_v"] // MIX_W), blk(MIX_W, _P_OFF["b_rg"] // MIX_W),
                  blk(LANES, _P_OFF["b_al"] // LANES), st,
                  full((LANES, MIX_W)), full((1, MIX_W)), full((1, MIX_W))],
        out_specs=[pl.BlockSpec((None, CHUNK, MIX_W), lambda b, c: (b, c, 0)), st],
        out_shape=[jax.ShapeDtypeStruct((bsz, t, MIX_W), BF16),
                   jax.ShapeDtypeStruct((bsz, B_HEADS, LANES, LANES), F32)],
        compiler_params=pltpu.CompilerParams(dimension_semantics=("parallel", "arbitrary"),
                                             vmem_limit_bytes=VMEM_LIMIT),
        name="gla",
    )(p3, p3, p3, p3, p3, s0, lw["b_alpha_up"], lw["b_alpha_bias"], lw["b_norm"])


def _band_kernel(q_ref, k_ref, v_ref, g_ref, y_ref, bias_scr, *, pad_rows):
    b = pl.program_id(0)
    c = pl.program_id(1)
    C = CHUNK

    @pl.when(jnp.logical_and(b == 0, c == 0))
    def _():
        rows = _iota2((C, C_WIN), 0)
        for h in range(C_HEADS):
            x = jnp.broadcast_to(g_ref[h:h + 1, :], (C, C_WIN))
            for bit in range(6):
                x = jnp.where(((rows >> bit) & 1) == 1, pltpu.roll(x, 1 << bit, 1), x)
            bias_scr[h] = x

    start = pl.multiple_of(c * C, C)
    col = _iota2((C, C_WIN), 1)
    valid = jnp.logical_and(col < C_BAND + C, col + c * C >= pad_rows)
    heads = range(C_HEADS)
    sls = [slice(h * LANES, (h + 1) * LANES) for h in heads]
    scores = [_mm_nt((q_ref[:, sl] * (C_HD ** -0.5)).astype(BF16), k_ref[pl.ds(start, C_WIN), sl]) for sl in sls]
    probs, denoms = [], []
    for h in heads:
        s = jnp.where(valid, scores[h] + bias_scr[h], -jnp.inf)
        p = jnp.exp(s - jnp.max(s, axis=-1, keepdims=True))
        denoms.append(jnp.sum(p, axis=-1, keepdims=True))
        probs.append(p.astype(BF16))
    outs = [_mm(probs[h], v_ref[pl.ds(start, C_WIN), sls[h]]) for h in heads]
    for h, sl in zip(heads, sls):
        y_ref[:, sl] = (outs[h] / denoms[h]).astype(BF16)


def _band(p3, kpad, vpad, gtab, pad_rows):
    bsz, t, _ = p3.shape
    hw = C_HEADS * LANES
    tp = kpad.shape[1]
    kv = pl.BlockSpec((None, tp, hw), lambda b, c: (b, 0, 0))
    return pl.pallas_call(
        functools.partial(_band_kernel, pad_rows=pad_rows),
        grid=(bsz, t // CHUNK),
        in_specs=[pl.BlockSpec((None, CHUNK, hw), lambda b, c: (b, c, _P_OFF["c_q"] // hw)), kv, kv,
                  pl.BlockSpec((C_HEADS, C_WIN), lambda b, c: (0, 0))],
        out_specs=pl.BlockSpec((None, CHUNK, hw), lambda b, c: (b, c, 0)),
        out_shape=jax.ShapeDtypeStruct((bsz, t, hw), BF16),
        scratch_shapes=[pltpu.VMEM((C_HEADS, CHUNK, C_WIN), F32)],
        compiler_params=pltpu.CompilerParams(dimension_semantics=("arbitrary", "arbitrary"),
                                             vmem_limit_bytes=VMEM_LIMIT),
        name="band_attn",
    )(p3, kpad, vpad, gtab)


def _dn_kernel(qkv_ref, ba_ref, gate_ref, cb0_ref, s0_ref, cw_ref, alog_ref, dtb_ref, nw_ref,
               y_ref, s_scr, carry_scr):
    c = pl.program_id(1)
    C = CHUNK

    @pl.when(c == 0)
    def _():
        carry_scr[...] = cb0_ref[...]
        s_scr[...] = s0_ref[...]

    x = qkv_ref[...]
    c0, c1, c2 = carry_scr[5:6, :], carry_scr[6:7, :], carry_scr[7:8, :]
    conv = (_shift_rows(x, 3, [c0, c1, c2]) * cw_ref[0:1, :] + _shift_rows(x, 2, [c1, c2]) * cw_ref[1:2, :]
            + _shift_rows(x, 1, [c2]) * cw_ref[2:3, :] + x * cw_ref[3:4, :])
    carry_scr[...] = x[C - 8:C, :]
    qkv = conv * _sigmoid(conv)

    ba = ba_ref[...]
    beta_all = _sigmoid(ba)
    g_all = -jnp.exp(alog_ref[...]) * _softplus(ba + dtb_ref[...])
    tri = _lower_tri_bf16(C)
    cum_all = _mm_exact_lhs(tri, g_all)
    lane = _iota2((C, LANES), 1)
    col_of = lambda arr, j: jnp.sum(jnp.where(lane == j, arr, 0.0), axis=-1, keepdims=True)
    ri, ci = _iota2((C, C), 0), _iota2((C, C), 1)
    after = (ri > ci).astype(F32)
    gate = gate_ref[...]

    heads = range(D_HEADS)
    sls = [slice(h * LANES, (h + 1) * LANES) for h in heads]
    betas = [col_of(beta_all, h) for h in heads]
    cums = [col_of(cum_all, D_HEADS + h) for h in heads]
    diffs = [_mm_exact_lhs(tri, col_of(g_all, D_HEADS + h) * after) for h in heads]
    decays = [jnp.where(ri >= ci, jnp.exp(diffs[h]), 0.0) for h in heads]
    qs, ks, vs = [], [], []
    for h in heads:
        q_h = qkv[:, h * LANES:(h + 1) * LANES]
        k_h = qkv[:, MIX_W + h * LANES:MIX_W + (h + 1) * LANES]
        qs.append(q_h * lax.rsqrt(jnp.sum(q_h * q_h, axis=-1, keepdims=True) + 1e-6) * (D_HD ** -0.5))
        ks.append(k_h * lax.rsqrt(jnp.sum(k_h * k_h, axis=-1, keepdims=True) + 1e-6))
        vs.append(qkv[:, 2 * MIX_W + h * LANES:2 * MIX_W + (h + 1) * LANES])
    kbf = [k.astype(BF16) for k in ks]
    kks = [_mm_nt(kbf[h], kbf[h]) for h in heads]
    atts = [_mm_nt(qs[h], kbf[h]) * decays[h] for h in heads]
    t_inv = _tri_inverse_many([jnp.where(ri > ci, -(betas[h] * decays[h] * kks[h]), 0.0) for h in heads])
    ecums = [jnp.exp(cums[h]) for h in heads]
    uws = [_mm(t_inv[h], jnp.concatenate([vs[h] * betas[h], ks[h] * (betas[h] * ecums[h])], axis=1)) for h in heads]
    s_old = [s_scr[h] for h in heads]
    deltas = [uws[h][:, 0:LANES] - _mm(uws[h][:, LANES:2 * LANES], s_old[h]) for h in heads]
    outs = [_mm(qs[h] * ecums[h], s_old[h]) + _mm(atts[h], deltas[h]) for h in heads]
    zeros = jnp.zeros((C, LANES), F32)
    for h in heads:
        cum_last = cums[h][C - 1:C, :]
        kdt = jnp.concatenate([ks[h] * jnp.exp(cum_last - cums[h]), zeros], axis=0).T
        s_scr[h] = s_old[h] * jnp.exp(cum_last) + _mm(kdt, jnp.concatenate([deltas[h], zeros], axis=0))
    for h, sl in zip(heads, sls):
        o = outs[h]
        o = o * lax.rsqrt(jnp.mean(o * o, axis=-1, keepdims=True) + NORM_EPS) * nw_ref[...]
        gt = gate[:, sl]
        y_ref[:, sl] = (o * (gt * _sigmoid(gt))).astype(BF16)


def _dn(p3, conv_buf, s0, lw):
    bsz, t, _ = p3.shape
    w3 = 3 * MIX_W
    full = lambda shp: pl.BlockSpec(shp, lambda b, c: (0,) * len(shp))
    st = pl.BlockSpec((None, D_HEADS, LANES, LANES), lambda b, c: (b, 0, 0, 0))
    return pl.pallas_call(
        _dn_kernel,
        grid=(bsz, t // CHUNK),
        in_specs=[pl.BlockSpec((None, CHUNK, w3), lambda b, c: (b, c, _P_OFF["d_qkv"] // w3)),
                  pl.BlockSpec((None, CHUNK, LANES), lambda b, c: (b, c, _P_OFF["d_ba"] // LANES)),
                  pl.BlockSpec((None, CHUNK, MIX_W), lambda b, c: (b, c, _P_OFF["d_gate"] // MIX_W)),
                  pl.BlockSpec((None, 8, w3), lambda b, c: (b, 0, 0)), st,
                  full((D_CONV, w3)), full((1, LANES)), full((1, LANES)), full((1, LANES))],
        out_specs=[pl.BlockSpec((None, CHUNK, MIX_W), lambda b, c: (b, c, 0)), st],
        out_shape=[jax.ShapeDtypeStruct((bsz, t, MIX_W), BF16),
                   jax.ShapeDtypeStruct((bsz, D_HEADS, LANES, LANES), F32)],
        scratch_shapes=[pltpu.VMEM((8, w3), F32)],
        compiler_params=pltpu.CompilerParams(dimension_semantics=("parallel", "arbitrary"),
                                             vmem_limit_bytes=VMEM_LIMIT),
        name="deltanet",
    )(p3, p3, p3, conv_buf, s0, lw["d_conv_w"], lw["d_a_log"], lw["d_dt_bias"], lw["d_norm"])


def _merge_kernel(u_ref, ya_ref, yb_ref, yc_ref, yd_ref, x_ref, wg_ref, wba_ref, wbb_ref, wbc_ref, wbd_ref,
                  wo_ref, gain_ref, h_ref, acc_ref):
    j = pl.program_id(1)

    @pl.when(j == 0)
    def _():
        acc_ref[...] = jnp.zeros_like(acc_ref)

    u = u_ref[...]
    merged = None
    for b, (y_r, wb_r) in enumerate(((ya_ref, wba_ref), (yb_ref, wbb_ref), (yc_ref, wbc_ref), (yd_ref, wbd_ref))):
        gate = _sigmoid(jnp.dot(u, wg_ref[b], preferred_element_type=F32))
        term = gate * jnp.dot(y_r[...], wb_r[...], preferred_element_type=F32)
        merged = term if merged is None else merged + term
    acc_ref[...] += jnp.dot(merged.astype(BF16), wo_ref[...], preferred_element_type=F32)

    @pl.when(j == pl.num_programs(1) - 1)
    def _():
        h_ref[...] = x_ref[...] + _rms(acc_ref[...], gain_ref[...])


def _merge(u, ya, yb, yc, yd, x2, lw):
    n = x2.shape[0]
    tm = min(n, 512)
    tn = 256
    row = lambda w: pl.BlockSpec((tm, w), lambda i, j: (i, 0))
    wcol = lambda k: pl.BlockSpec((k, tn), lambda i, j: (0, j))
    return pl.pallas_call(
        _merge_kernel,
        grid=(n // tm, D_MODEL // tn),
        in_specs=[row(D_MODEL), row(ya.shape[1]), row(yb.shape[1]), row(yc.shape[1]), row(yd.shape[1]),
                  row(D_MODEL),
                  pl.BlockSpec((4, D_MODEL, tn), lambda i, j: (0, 0, j)),
                  wcol(ya.shape[1]), wcol(yb.shape[1]), wcol(yc.shape[1]), wcol(yd.shape[1]),
                  pl.BlockSpec((tn, D_MODEL), lambda i, j: (j, 0)),
                  pl.BlockSpec((1, D_MODEL), lambda i, j: (0, 0))],
        out_specs=row(D_MODEL),
        out_shape=jax.ShapeDtypeStruct((n, D_MODEL), F32),
        scratch_shapes=[pltpu.VMEM((tm, D_MODEL), F32)],
        compiler_params=pltpu.CompilerParams(dimension_semantics=("parallel", "arbitrary"),
                                             vmem_limit_bytes=VMEM_LIMIT),
        name="merge",
    )(u, ya, yb, yc, yd, x2, lw["w_merge_gate"], lw["wb_a"], lw["wb_b"], lw["wb_c"], lw["wb_d"],
      lw["w_out"], lw["norm_mix_post"])


def _ffn_kernel(h_ref, g1_ref, wu_ref, wd_ref, g2_ref, o_ref, z_scr, acc_ref):
    j = pl.program_id(1)

    @pl.when(j == 0)
    def _():
        z_scr[...] = _rms(h_ref[...], g1_ref[...]).astype(BF16)
        acc_ref[...] = jnp.zeros_like(acc_ref)

    a = jnp.maximum(jnp.dot(z_scr[...], wu_ref[...], preferred_element_type=F32), 0.0)
    acc_ref[...] += jnp.dot((a * a).astype(BF16), wd_ref[...], preferred_element_type=F32)

    @pl.when(j == pl.num_programs(1) - 1)
    def _():
        o_ref[...] = h_ref[...] + _rms(acc_ref[...], g2_ref[...])


def _ffn(h, lw):
    n = h.shape[0]
    tm = min(n, 512)
    tf = 1024
    vec = pl.BlockSpec((1, D_MODEL), lambda i, j: (0, 0))
    return pl.pallas_call(
        _ffn_kernel,
        grid=(n // tm, D_FF // tf),
        in_specs=[pl.BlockSpec((tm, D_MODEL), lambda i, j: (i, 0)), vec,
                  pl.BlockSpec((D_MODEL, tf), lambda i, j: (0, j)),
                  pl.BlockSpec((tf, D_MODEL), lambda i, j: (j, 0)), vec],
        out_specs=pl.BlockSpec((tm, D_MODEL), lambda i, j: (i, 0)),
        out_shape=jax.ShapeDtypeStruct((n, D_MODEL), F32),
        scratch_shapes=[pltpu.VMEM((tm, D_MODEL), BF16), pltpu.VMEM((tm, D_MODEL), F32)],
        compiler_params=pltpu.CompilerParams(dimension_semantics=("parallel", "arbitrary"),
                                             vmem_limit_bytes=VMEM_LIMIT),
        name="ffn",
    )(h, lw["norm_ffn_pre"], lw["w_ffn_up"], lw["w_ffn_down"], lw["norm_ffn_post"])


def _band_bias_row(rel):
    m = np.arange(C_WIN)
    idx = np.where(m <= C_BAND + CHUNK, np.clip(C_BAND - m, -C_REL_CLIP, C_REL_CLIP) + C_REL_CLIP, 2 * C_REL_CLIP)
    return rel[:, idx.astype(np.int32)]


def _prep_layer(l, w):
    row = lambda v: v.reshape(1, -1)
    a_slot = lambda v: _take_cols(v, _A_SLOT_IDX)
    zeros64 = jnp.zeros((64, A_HEADS * LANES), F32)
    d_lane = np.full(LANES, -1, np.int32)
    d_lane[D_HEADS:2 * D_HEADS] = np.arange(D_HEADS)
    wb = w["w_branch"][l]
    return {
        "norm_mix_pre": row(w["norm_mix_pre"][l]), "norm_mix_post": row(w["norm_mix_post"][l]),
        "norm_ffn_pre": row(w["norm_ffn_pre"][l]), "norm_ffn_post": row(w["norm_ffn_post"][l]),
        "w_in": _take_cols(w["w_in"][l], _P_IDX).astype(BF16),
        "w_merge_gate": w["w_merge_gate"][l].astype(BF16),
        "wb_a": _take_rows(wb[0], _A_SLOT_IDX).astype(BF16), "wb_b": wb[1].astype(BF16),
        "wb_c": _take_rows(wb[2], _A_SLOT_IDX).astype(BF16), "wb_d": wb[3].astype(BF16),
        "w_out": w["w_out"][l].astype(BF16),
        "w_ffn_up": w["w_ffn_up"][l].astype(BF16), "w_ffn_down": w["w_ffn_down"][l].astype(BF16),
        "a_mu": _take_cols(row(w["a_mu"][l]), _A_ROW_IDX),
        "a_w0": a_slot(row(w["a_w0"][l])), "a_a0": a_slot(row(w["a_a0"][l])),
        "a_w_up": jnp.concatenate([a_slot(w["a_w_up"][l]), zeros64], axis=0).astype(BF16),
        "a_a_up": jnp.concatenate([zeros64, a_slot(w["a_a_up"][l])], axis=0).astype(BF16),
        "a_g_up": a_slot(w["a_g_up"][l]).astype(BF16),
        "a_k_k": a_slot(row(w["a_k_k"][l])), "a_k_a": a_slot(row(w["a_k_a"][l])),
        "a_r_k": a_slot(row(w["a_r_k"][l])),
        "a_ln_w": a_slot(row(w["a_ln_w"][l])), "a_ln_b": a_slot(row(w["a_ln_b"][l])),
        "b_alpha_up": jnp.pad(_take_cols(w["b_alpha_up"][l], _B_SLOT_IDX), ((0, LANES - 16), (0, 0))).astype(BF16),
        "b_alpha_bias": _take_cols(row(w["b_alpha_bias"][l]), _B_SLOT_IDX),
        "b_norm": row(w["b_norm"][l]),
        "c_gtab": _band_bias_row(w["c_rel_bias"][l]),
        "d_conv_w": w["d_conv_w"][l],
        "d_a_log": _take_cols(row(w["d_a_log"][l]), d_lane), "d_dt_bias": _take_cols(row(w["d_dt_bias"][l]), d_lane),
        "d_norm": row(w["d_norm"][l]),
    }


def _layer(x, st, lw):
    bsz, t, _ = x.shape
    n = bsz * t
    shift_prev, s_a, s_b, band_k, band_v, conv_buf, s_d = st
    x2 = x.reshape(n, D_MODEL)
    u, p = _inproj(x2, lw["norm_mix_pre"], lw["w_in"])
    p3 = p.reshape(bsz, t, P_COLS)

    def seg(name, width):
        return p3[:, :, _P_OFF[name]:_P_OFF[name] + width]

    y_a, s_a_new = _rwkv(p3, _take_cols(shift_prev, _A_ROW_IDX),
                         jnp.pad(s_a, ((0, 0), (0, 0), (0, LANES - A_HD), (0, LANES - A_HD))), lw)
    s_b_t = jnp.pad(jnp.swapaxes(s_b, -1, -2), ((0, 0), (0, 0), (0, 0), (0, LANES - B_DK)))
    y_b, s_b_new = _gla(p3, s_b_t, lw)
    hw = C_HEADS * LANES
    k_new, v_new = seg("c_k", hw), seg("c_v", hw)
    if band_k is None:
        pad_rows = C_BAND
        front = jnp.zeros((bsz, C_BAND, hw), BF16)
    else:
        pad_rows = 0
        slot = lambda z: jnp.pad(z, ((0, 0), (0, 0), (0, 0), (0, LANES - C_HD))).reshape(bsz, C_BAND, hw).astype(BF16)
        front = None
    tail = jnp.zeros((bsz, CHUNK, hw), BF16)
    if band_k is None:
        kpad = jnp.pad(k_new.astype(BF16), ((0, 0), (C_BAND, CHUNK), (0, 0)))
        vpad = jnp.pad(v_new.astype(BF16), ((0, 0), (C_BAND, CHUNK), (0, 0)))
    else:
        kpad = jnp.concatenate([slot(band_k), k_new.astype(BF16), tail], axis=1)
        vpad = jnp.concatenate([slot(band_v), v_new.astype(BF16), tail], axis=1)
    y_c = _band(p3, kpad, vpad, lw["c_gtab"], pad_rows)
    y_d, s_d_new = _dn(p3, jnp.pad(conv_buf, ((0, 0), (8 - (D_CONV - 1), 0), (0, 0))), s_d, lw)

    h = _merge(u, y_a.reshape(n, -1), y_b.reshape(n, -1), y_c.reshape(n, -1), y_d.reshape(n, -1), x2, lw)
    out = _ffn(h, lw).reshape(bsz, t, D_MODEL)

    keep = min(C_BAND, t)
    unslot = lambda z: z.reshape(bsz, t, C_HEADS, LANES)[:, t - keep:, :, :C_HD]
    new_state = (
        _take_cols(p3[:, t - 1:, :], _A_ORIG_POS),
        s_a_new[:, :, :A_HD, :A_HD],
        jnp.swapaxes(s_b_new[:, :, :, :B_DK], -1, -2),
        unslot(k_new), unslot(v_new),
        seg("d_qkv", 3 * MIX_W)[:, t - (D_CONV - 1):, :],
        s_d_new,
    )
    return out, new_state


def kernel(x_prompt, x_sample, state_rwkv_shift, state_rwkv, state_gla, cache_band_k, cache_band_v, state_dn_conv, state_dn, norm_mix_pre, norm_mix_post, norm_ffn_pre, norm_ffn_post, w_in, w_merge_gate, w_branch, w_out, w_ffn_up, w_ffn_down, a_mu, a_w0, a_w_up, a_a0, a_a_up, a_g_up, a_k_k, a_k_a, a_r_k, a_ln_w, a_ln_b, b_alpha_up, b_alpha_bias, b_norm, c_rel_bias, d_conv_w, d_a_log, d_dt_bias, d_norm):
    weights = dict(norm_mix_pre=norm_mix_pre, norm_mix_post=norm_mix_post, norm_ffn_pre=norm_ffn_pre,
                   norm_ffn_post=norm_ffn_post, w_in=w_in, w_merge_gate=w_merge_gate, w_branch=w_branch,
                   w_out=w_out, w_ffn_up=w_ffn_up, w_ffn_down=w_ffn_down, a_mu=a_mu, a_w0=a_w0, a_w_up=a_w_up,
                   a_a0=a_a0, a_a_up=a_a_up, a_g_up=a_g_up, a_k_k=a_k_k, a_k_a=a_k_a,
                   a_r_k=a_r_k.reshape(a_r_k.shape[0], -1), a_ln_w=a_ln_w, a_ln_b=a_ln_b, b_alpha_up=b_alpha_up,
                   b_alpha_bias=b_alpha_bias, b_norm=b_norm, c_rel_bias=c_rel_bias, d_conv_w=d_conv_w,
                   d_a_log=d_a_log, d_dt_bias=d_dt_bias, d_norm=d_norm)
    bsz = x_prompt.shape[0]
    depth = w_in.shape[0]
    y_p, y_s = x_prompt, x_sample
    new_p, new_s = [], []
    for l in range(depth):
        lw = _prep_layer(l, weights)
        zero_state = (jnp.zeros((bsz, 1, A_IN), F32), jnp.zeros((bsz, A_HEADS, A_HD, A_HD), F32),
                      jnp.zeros((bsz, B_HEADS, B_DK, B_DV), F32), None, None,
                      jnp.zeros((bsz, D_CONV - 1, 3 * MIX_W), F32), jnp.zeros((bsz, D_HEADS, D_HD, D_HD), F32))
        y_p, st_p = _layer(y_p, zero_state, lw)
        y_s, st_s = _layer(y_s, (state_rwkv_shift[l], state_rwkv[l], state_gla[l], cache_band_k[l],
                                 cache_band_v[l], state_dn_conv[l], state_dn[l]), lw)
        new_p.append(st_p)
        new_s.append(st_s)
    stack = lambda states, i: jnp.stack([s[i] for s in states])
    return ((y_p, y_s) + tuple(stack(new_p, i) for i in range(7)) + tuple(stack(new_s, i) for i in range(7)))
```
